```python
import math
import jax, jax.numpy as jnp
from jax import lax
import numpy as np

D_MODEL = 2048
BATCH = 2
SEQ = 4096
DEPTH = 2
DEC_BATCH = 32
DEC_SEQ = 4
PAST_LEN = 8192
PAGE_SIZE = 128

HEAD_DIM = 128
H_FOX = (D_MODEL // 2) // HEAD_DIM
H_DIFF = (D_MODEL // 2) // HEAD_DIM
DIFF_DK = HEAD_DIM // 2
H_SB = D_MODEL // HEAD_DIM
FOX_W = H_FOX * HEAD_DIM
DIFF_W = H_DIFF * 2 * DIFF_DK
SB_W = H_SB * HEAD_DIM
IN_EVEN = 3 * FOX_W + H_FOX + 3 * DIFF_W
IN_ODD = 3 * SB_W
EVEN_SPLITS = (FOX_W, 2 * FOX_W, 3 * FOX_W, 3 * FOX_W + H_FOX,
               3 * FOX_W + H_FOX + DIFF_W, 3 * FOX_W + H_FOX + 2 * DIFF_W)
N_BUCKETS = 32
MAX_DISTANCE = 128
D_FF = 5632
N_EXPERTS = 8
TOP_K = 2
D_FF_EXPERT = 5632
Q_BLOCK = 128
EPS = 1e-6
NEG = -1e30
N_EVEN = (DEPTH + 1) // 2
N_ODD = DEPTH // 2

kernel_name = "hybrid_fox_diff_stickbreak_adaln_decoder_step"


def _rms(x, g=None):
    xf = x.astype(jnp.float32)
    y = xf * lax.rsqrt(jnp.mean(xf * xf, axis=-1, keepdims=True) + EPS)
    if g is not None:
        y = y * g.astype(jnp.float32)
    return y.astype(x.dtype)


def _modulate(hn, shift, scale):
    return hn * (1 + scale[:, None, :]) + shift[:, None, :]


def _swiglu(h, wg, wu, wd):
    return (jax.nn.silu(h @ wg) * (h @ wu)) @ wd


def _t5_bucket(rel):
    n = jnp.maximum(rel, 0)
    exact = N_BUCKETS // 2
    large = exact + (jnp.log(jnp.maximum(n, 1).astype(jnp.float32) / exact)
                     / math.log(MAX_DISTANCE / exact) * (N_BUCKETS - exact)).astype(jnp.int32)
    return jnp.where(n < exact, n, jnp.minimum(large, N_BUCKETS - 1))


def _sweep(block_fn, q_args, q_pos):
    T = q_pos.shape[0]
    if T <= Q_BLOCK:
        return block_fn(q_args, q_pos)
    nb = T // Q_BLOCK
    blocked = tuple(jnp.moveaxis(a.reshape(a.shape[0], nb, Q_BLOCK, *a.shape[2:]), 1, 0) for a in q_args)
    out = lax.map(lambda bq: block_fn(bq[0], bq[1]), (blocked, q_pos.reshape(nb, Q_BLOCK)))
    out = jnp.moveaxis(out, 0, 1)
    return out.reshape(out.shape[0], T, *out.shape[3:])


def _logits(q, k_parts, eq):
    return jnp.concatenate([jnp.einsum(eq, q, k) for k in k_parts], axis=-1).astype(jnp.float32)


def _mix_values(p, v_parts):
    out, off = 0, 0
    for v in v_parts:
        n = v.shape[1]
        out = out + jnp.einsum('bhqk,bkhd->bqhd', p[..., off:off + n], v)
        off += n
    return out


def _fox_attend(q, cq, k_parts, v_parts, ck, q_pos, k_pos):
    s = _logits(q, k_parts, 'bqhd,bkhd->bhqk') * HEAD_DIM ** -0.5
    s = s + jnp.swapaxes(cq, 1, 2)[..., :, None] - jnp.swapaxes(ck, 1, 2)[..., None, :]
    s = jnp.where(k_pos[None, :] <= q_pos[:, None], s, NEG)
    p = jax.nn.softmax(s, axis=-1).astype(v_parts[0].dtype)
    return _mix_values(p, v_parts)


def _diff_attend(q, k_parts, v_parts, rel_bias, lam, lam_init, q_pos, k_pos):
    s = _logits(q, k_parts, 'bqhcd,bkhcd->bchqk') * DIFF_DK ** -0.5
    bias = jnp.moveaxis(rel_bias[_t5_bucket(q_pos[:, None] - k_pos[None, :])], -1, 0)
    s = s + bias.astype(jnp.float32)
    s = jnp.where(k_pos[None, :] <= q_pos[:, None], s, NEG)
    p = jax.nn.softmax(s, axis=-1)
    a = (p[:, 0] - lam * p[:, 1]).astype(v_parts[0].dtype)
    o = _mix_values(a, v_parts)
    return _rms(o) * (1.0 - lam_init)


def _sb_attend(q, k_parts, v_parts, q_pos, k_pos):
    z = _logits(q, k_parts, 'bqhd,bkhd->bhqk') * HEAD_DIM ** -0.5
    mask = k_pos[None, :] < q_pos[:, None]
    log_rest = jnp.where(mask, jax.nn.log_sigmoid(-z), 0.0)
    after = lax.cumsum(log_rest, axis=3, reverse=True) - log_rest
    a = jnp.where(mask, jnp.exp(jax.nn.log_sigmoid(z) + after), 0.0).astype(v_parts[0].dtype)
    return _mix_values(a, v_parts)


def _even_mixer(h, past, q_pos, k_pos, w_in, b_f, g_qf, g_kf, g_qd, g_kd, lam_p, lam_init, rel_bias, w_out):
    B, T, _ = h.shape
    qf, kf, vf, fg, qd, kd, vd = jnp.split(h @ w_in, EVEN_SPLITS, axis=-1)
    qf = _rms(qf.reshape(B, T, H_FOX, HEAD_DIM), g_qf)
    kf = _rms(kf.reshape(B, T, H_FOX, HEAD_DIM), g_kf)
    vf = vf.reshape(B, T, H_FOX, HEAD_DIM)
    logf = jax.nn.log_sigmoid(fg.astype(jnp.float32) + b_f.astype(jnp.float32))
    qd = _rms(qd.reshape(B, T, H_DIFF, 2, DIFF_DK), g_qd)
    kd = _rms(kd.reshape(B, T, H_DIFF, 2, DIFF_DK), g_kd)
    vd = vd.reshape(B, T, H_DIFF, 2 * DIFF_DK)
    new_fox_kv = jnp.stack([kf, vf], axis=2)
    new_diff_kv = jnp.stack([kd.reshape(B, T, H_DIFF, 2 * DIFF_DK), vd], axis=2)
    if past is None:
        kf_parts, vf_parts, logf_all = (kf,), (vf,), logf
        kd_parts, vd_parts = (kd,), (vd,)
    else:
        pf_kv, pf_logf, pd_kv = past
        P = pf_kv.shape[1]
        kf_parts, vf_parts = (pf_kv[:, :, 0], kf), (pf_kv[:, :, 1], vf)
        logf_all = jnp.concatenate([pf_logf.astype(jnp.float32), logf], axis=1)
        kd_parts = (pd_kv[:, :, 0].reshape(B, P, H_DIFF, 2, DIFF_DK), kd)
        vd_parts = (pd_kv[:, :, 1], vd)
    ck = jnp.cumsum(logf_all, axis=1)
    cq = ck[:, ck.shape[1] - T:]
    lp = lam_p.astype(jnp.float32)
    lam = jnp.exp(jnp.sum(lp[0] * lp[1])) - jnp.exp(jnp.sum(lp[2] * lp[3])) + lam_init
    o_fox = _sweep(lambda a, qp: _fox_attend(a[0], a[1], kf_parts, vf_parts, ck, qp, k_pos), (qf, cq), q_pos)
    o_diff = _sweep(lambda a, qp: _diff_attend(a[0], kd_parts, vd_parts, rel_bias, lam, lam_init, qp, k_pos), (qd,), q_pos)
    o = jnp.concatenate([o_fox.reshape(B, T, FOX_W), o_diff.reshape(B, T, DIFF_W)], axis=-1)
    return o @ w_out, (new_fox_kv, logf, new_diff_kv)


def _odd_mixer(h, past, q_pos, k_pos, w_in, w_out):
    B, T, _ = h.shape
    q, k, v = [a.reshape(B, T, H_SB, HEAD_DIM) for a in jnp.split(h @ w_in, 3, axis=-1)]
    new_kv = jnp.stack([k, v], axis=2)
    if past is None:
        k_parts, v_parts = (k,), (v,)
    else:
        k_parts, v_parts = (past[:, :, 0], k), (past[:, :, 1], v)
    o = _sweep(lambda a, qp: _sb_attend(a[0], k_parts, v_parts, qp, k_pos), (q,), q_pos)
    return o.reshape(B, T, SB_W) @ w_out, new_kv


def _moe(h, w_router, b_router, wg, wu, wd):
    logits = (h @ w_router).astype(jnp.float32) + b_router.astype(jnp.float32)
    top_v, top_i = lax.top_k(logits, TOP_K)
    top_w = jax.nn.softmax(top_v, axis=-1)
    gates = jnp.einsum('btk,btke->bte', top_w, jax.nn.one_hot(top_i, N_EXPERTS, dtype=jnp.float32)).astype(h.dtype)
    out = jnp.zeros_like(h)
    for e in range(N_EXPERTS):
        out = out + gates[..., e:e + 1] * _swiglu(h, wg[e], wu[e], wd[e])
    return out


def setup_inputs(seed: int = 0) -> dict:
    key = jax.random.key(seed)
    ks = iter(jax.random.split(key, 48))
    nrm = lambda shape, scale=1.0: jax.random.normal(next(ks), shape, jnp.float32) * scale
    inv = D_MODEL ** -0.5
    n_pages = PAST_LEN // PAGE_SIZE
    n_phys = (DEC_BATCH * n_pages * 5) // 4
    page_table = jax.random.permutation(next(ks), n_phys)[:DEC_BATCH * n_pages].reshape(DEC_BATCH, n_pages).astype(jnp.int32)
    return {
        "x_prompt": nrm((BATCH, SEQ, D_MODEL)),
        "x_sample": nrm((DEC_BATCH, DEC_SEQ, D_MODEL)),
        "cache_fox_kv": nrm((N_EVEN, n_phys, PAGE_SIZE, 2, H_FOX, HEAD_DIM)),
        "cache_fox_logf": jax.nn.log_sigmoid(3.0 + nrm((N_EVEN, n_phys, PAGE_SIZE, H_FOX), 0.5)),
        "cache_diff_kv": nrm((N_EVEN, n_phys, PAGE_SIZE, 2, H_DIFF, 2 * DIFF_DK)),
        "cache_sb_kv": nrm((N_ODD, n_phys, PAGE_SIZE, 2, H_SB, HEAD_DIM)),
        "page_table": page_table,
        "c_prompt": nrm((BATCH, D_MODEL)),
        "c_sample": nrm((DEC_BATCH, D_MODEL)),
        "w_ada": nrm((DEPTH, D_MODEL, 6 * D_MODEL), 0.5 * inv),
        "b_ada": nrm((DEPTH, 6 * D_MODEL), 0.02),
        "g_norm": 1.0 + nrm((DEPTH, 2, D_MODEL), 0.02),
        "w_in_even": nrm((N_EVEN, D_MODEL, IN_EVEN), inv),
        "b_forget": 3.0 + nrm((N_EVEN, H_FOX), 0.5),
        "g_q_fox": 1.0 + nrm((N_EVEN, HEAD_DIM), 0.02),
        "g_k_fox": 1.0 + nrm((N_EVEN, HEAD_DIM), 0.02),
        "g_q_diff": 1.0 + nrm((N_EVEN, DIFF_DK), 0.02),
        "g_k_diff": 1.0 + nrm((N_EVEN, DIFF_DK), 0.02),
        "lam_diff": nrm((N_EVEN, 4, DIFF_DK), 0.1),
        "rel_bias": nrm((N_BUCKETS, H_DIFF), 0.5),
        "w_out_even": nrm((N_EVEN, FOX_W + DIFF_W, D_MODEL), (FOX_W + DIFF_W) ** -0.5),
        "w_gate": nrm((N_EVEN, D_MODEL, D_FF), inv),
        "w_up": nrm((N_EVEN, D_MODEL, D_FF), inv),
        "w_down": nrm((N_EVEN, D_FF, D_MODEL), D_FF ** -0.5),
        "w_in_odd": nrm((N_ODD, D_MODEL, IN_ODD), inv),
        "w_out_odd": nrm((N_ODD, SB_W, D_MODEL), SB_W ** -0.5),
        "w_router": nrm((N_ODD, D_MODEL, N_EXPERTS), inv),
        "b_router": nrm((N_ODD, N_EXPERTS), 0.01),
        "w_gate_exp": nrm((N_ODD, N_EXPERTS, D_MODEL, D_FF_EXPERT), inv),
        "w_up_exp": nrm((N_ODD, N_EXPERTS, D_MODEL, D_FF_EXPERT), inv),
        "w_down_exp": nrm((N_ODD, N_EXPERTS, D_FF_EXPERT, D_MODEL), D_FF_EXPERT ** -0.5),
    }


def reference(x_prompt, x_sample, cache_fox_kv, cache_fox_logf, cache_diff_kv, cache_sb_kv, page_table,
              c_prompt, c_sample, w_ada, b_ada, g_norm, w_in_even, b_forget, g_q_fox, g_k_fox,
              g_q_diff, g_k_diff, lam_diff, rel_bias, w_out_even, w_gate, w_up, w_down,
              w_in_odd, w_out_odd, w_router, b_router, w_gate_exp, w_up_exp, w_down_exp):
    def run(x, c, gather, q_pos, k_pos):
        fox_kv, fox_logf, diff_kv, sb_kv = [], [], [], []
        for layer in range(DEPTH):
            i = layer // 2
            sh1, sc1, g1, sh2, sc2, g2 = jnp.split(jax.nn.silu(c) @ w_ada[layer] + b_ada[layer], 6, axis=-1)
            h = _modulate(_rms(x, g_norm[layer, 0]), sh1, sc1)
            if layer % 2 == 0:
                past = None if gather is None else (gather(cache_fox_kv[i]), gather(cache_fox_logf[i]), gather(cache_diff_kv[i]))
                lam_init = 0.8 - 0.6 * math.exp(-0.3 * layer)
                o, (fk, fl, dk) = _even_mixer(h, past, q_pos, k_pos, w_in_even[i], b_forget[i], g_q_fox[i], g_k_fox[i],
                                              g_q_diff[i], g_k_diff[i], lam_diff[i], lam_init, rel_bias, w_out_even[i])
                fox_kv.append(fk)
                fox_logf.append(fl)
                diff_kv.append(dk)
            else:
                past = None if gather is None else gather(cache_sb_kv[i])
                o, sk = _odd_mixer(h, past, q_pos, k_pos, w_in_odd[i], w_out_odd[i])
                sb_kv.append(sk)
            x = x + g1[:, None, :] * o
            h = _modulate(_rms(x, g_norm[layer, 1]), sh2, sc2)
            if layer % 2 == 0:
                f = _swiglu(h, w_gate[i], w_up[i], w_down[i])
            else:
                f = _moe(h, w_router[i], b_router[i], w_gate_exp[i], w_up_exp[i], w_down_exp[i])
            x = x + g2[:, None, :] * f
        return x, jnp.stack(fox_kv), jnp.stack(fox_logf), jnp.stack(diff_kv), jnp.stack(sb_kv)

    t_p = x_prompt.shape[1]
    pos_p = jnp.arange(t_p, dtype=jnp.int32)
    y_prompt, fkv_p, flf_p, dkv_p, skv_p = run(x_prompt, c_prompt, None, pos_p, pos_p)

    db, n_pages = page_table.shape
    past_len = n_pages * PAGE_SIZE
    t_s = x_sample.shape[1]
    gather = lambda cache: cache[page_table].reshape(db, past_len, *cache.shape[2:])
    k_pos_s = jnp.arange(past_len + t_s, dtype=jnp.int32)
    q_pos_s = past_len + jnp.arange(t_s, dtype=jnp.int32)
    y_sample, fkv_s, flf_s, dkv_s, skv_s = run(x_sample, c_sample, gather, q_pos_s, k_pos_s)

    return (y_prompt, y_sample, fkv_p, flf_p, dkv_p, skv_p, fkv_s, flf_s, dkv_s, skv_s)
```

```python
import functools
import math

import numpy as np
import jax
import jax.numpy as jnp
from jax import lax
from jax.experimental import pallas as pl
from jax.experimental.pallas import tpu as pltpu

F32 = jnp.float32
BF16 = jnp.bfloat16
I32 = jnp.int32

EPS = 1e-6
NEG = -1e30
LANE = 128
HEAD_DIM = 128
DIFF_DK = 64
N_BUCKETS = 32
MAX_DISTANCE = 128
N_EXPERTS = 8
VMEM_LIMIT = 56 * 1024 * 1024

NN = ((1,), (0,))
NT = ((1,), (1,))


def _cparams(sem):
    return pltpu.CompilerParams(dimension_semantics=sem, vmem_limit_bytes=VMEM_LIMIT)


def _dot(a, b, dims=NN):
    return lax.dot_general(a, b, (dims, ((), ())), preferred_element_type=F32)


def _split2(a):
    hi = a.astype(BF16)
    lo = (a - hi.astype(F32)).astype(BF16)
    return hi, lo


def _split3(a):
    hi = a.astype(BF16)
    r = a - hi.astype(F32)
    mid = r.astype(BF16)
    lo = (r - mid.astype(F32)).astype(BF16)
    return hi, mid, lo


def _dot_sel(sel_bf16, x, dims=NN, sel_first=True):
    acc = None
    for part in _split3(x.astype(F32)):
        t = _dot(sel_bf16, part, dims) if sel_first else _dot(part, sel_bf16, dims)
        acc = t if acc is None else acc + t
    return acc


def _mm(a, w):
    return _dot(a.astype(BF16), w.astype(BF16))


def _log_sigmoid(x):
    return jnp.minimum(x, 0.0) - jnp.log1p(jnp.exp(-jnp.abs(x)))


def _silu(x):
    return x / (1.0 + jnp.exp(-x))


def _ada_kernel(c_ref, w_ref, b_ref, o_ref):
    o_ref[...] = _mm(_silu(c_ref[...]), w_ref[...]) + b_ref[...]


def _ada_call(c_all, w_ada, b_ada, tn=1024):
    n_layers, d, n = w_ada.shape
    mc = c_all.shape[0]
    return pl.pallas_call(
        _ada_kernel,
        grid=(n_layers, n // tn),
        in_specs=[
            pl.BlockSpec((mc, d), lambda l, j: (0, 0)),
            pl.BlockSpec((None, d, tn), lambda l, j: (l, 0, j)),
            pl.BlockSpec((None, 1, tn), lambda l, j: (l, 0, j)),
        ],
        out_specs=pl.BlockSpec((None, mc, tn), lambda l, j: (l, 0, j)),
        out_shape=jax.ShapeDtypeStruct((n_layers, mc, n), F32),
        compiler_params=_cparams(("arbitrary", "arbitrary")),
        name="ada",
    )(c_all, w_ada, b_ada.reshape(n_layers, 1, n))


def _normmod_kernel(x_ref, g_ref, sc_ref, sh_ref, o_ref):
    x = x_ref[...]
    y = x * lax.rsqrt(jnp.mean(x * x, axis=-1, keepdims=True) + EPS)
    y = y * g_ref[...]
    o_ref[...] = (y * (1.0 + sc_ref[...]) + sh_ref[...]).astype(o_ref.dtype)


def _normmod_call(x, g, sc, sh, out_dtype, rows_per_batch, tm):
    m, d = x.shape
    if rows_per_batch % tm == 0:
        per = rows_per_batch // tm
        sc_in, sh_in = sc.reshape(-1, 1, d), sh.reshape(-1, 1, d)
        mod_spec = pl.BlockSpec((None, 1, d), lambda i: (i // per, 0, 0))
    else:
        sc_in = jnp.repeat(sc, rows_per_batch, axis=0)
        sh_in = jnp.repeat(sh, rows_per_batch, axis=0)
        mod_spec = pl.BlockSpec((tm, d), lambda i: (i, 0))
    return pl.pallas_call(
        _normmod_kernel,
        grid=(m // tm,),
        in_specs=[
            pl.BlockSpec((tm, d), lambda i: (i, 0)),
            pl.BlockSpec((1, d), lambda i: (0, 0)),
            mod_spec,
            mod_spec,
        ],
        out_specs=pl.BlockSpec((tm, d), lambda i: (i, 0)),
        out_shape=jax.ShapeDtypeStruct((m, d), out_dtype),
        compiler_params=_cparams(("arbitrary",)),
        name="normmod",
    )(x, g.reshape(1, d), sc_in, sh_in)


def _group_rms(y, gain, group):
    outs = []
    lane = lax.broadcasted_iota(I32, (1, LANE), 1)
    for c in range(y.shape[1] // LANE):
        blk = y[:, c * LANE:(c + 1) * LANE]
        sq = blk * blk
        if group == LANE:
            ms = jnp.mean(sq, axis=-1, keepdims=True)
        else:
            lo_half = lane < group
            s0 = jnp.sum(jnp.where(lo_half, sq, 0.0), axis=-1, keepdims=True)
            s1 = jnp.sum(jnp.where(lo_half, 0.0, sq), axis=-1, keepdims=True)
            ms = jnp.where(lo_half, s0, s1) * (1.0 / group)
        outs.append(blk * lax.rsqrt(ms + EPS) * gain[:, c * LANE:(c + 1) * LANE])
    return jnp.concatenate(outs, axis=-1) if len(outs) > 1 else outs[0]


def _inproj_kernel(a_ref, w_ref, gain_ref, *out_refs, group, n_norm_tiles, n_q_tiles,
                   with_bf16):
    j = pl.program_id(1)
    y = _mm(a_ref[...], w_ref[...])
    if with_bf16:
        qkv_ref, kv_ref = out_refs
    else:
        (qkv_ref,), kv_ref = out_refs, None

    def store(val):
        qkv_ref[...] = val.astype(qkv_ref.dtype)
        if kv_ref is not None:
            @pl.when(j >= n_q_tiles)
            def _():
                kv_ref[...] = val

    if n_norm_tiles == 0:
        store(y)
    else:
        @pl.when(j < n_norm_tiles)
        def _():
            store(_group_rms(y, gain_ref[...], group))

        @pl.when(j >= n_norm_tiles)
        def _():
            store(y)


def _inproj_call(h, w, col_off, n_cols, gain, group, with_bf16, tm, tn=512):
    m, k = h.shape
    assert col_off % tn == 0 and n_cols % (3 * tn) == 0
    n_tiles = n_cols // tn
    n_q_tiles = n_tiles // 3
    n_norm_tiles = 0 if gain is None else 2 * n_q_tiles
    if gain is None:
        gain = jnp.ones((1, n_cols), F32)
    off = col_off // tn
    kern = functools.partial(_inproj_kernel, group=group, n_norm_tiles=n_norm_tiles,
                             n_q_tiles=n_q_tiles, with_bf16=with_bf16)
    if with_bf16:
        out_specs = [pl.BlockSpec((tm, tn), lambda i, j: (i, j)),
                     pl.BlockSpec((tm, tn), lambda i, j: (i, jnp.maximum(j - n_q_tiles, 0)))]
        out_shape = [jax.ShapeDtypeStruct((m, n_cols), BF16),
                     jax.ShapeDtypeStruct((m, n_cols - n_cols // 3), F32)]
    else:
        out_specs = [pl.BlockSpec((tm, tn), lambda i, j: (i, j))]
        out_shape = [jax.ShapeDtypeStruct((m, n_cols), F32)]
    res = pl.pallas_call(
        kern,
        grid=(m // tm, n_tiles),
        in_specs=[
            pl.BlockSpec((tm, k), lambda i, j: (i, 0)),
            pl.BlockSpec((k, tn), lambda i, j: (0, j + off)),
            pl.BlockSpec((1, tn), lambda i, j: (0, j)),
        ],
        out_specs=out_specs,
        out_shape=out_shape,
        compiler_params=_cparams(("arbitrary", "arbitrary")),
        name="inproj",
    )(h, w, gain)
    return res if with_bf16 else res[0]


def _logf_kernel(a_ref, w_ref, b_ref, lf_ref, ck_ref, suf_ref, carry_ref, *, seg):
    i = pl.program_id(0)
    tm = a_ref.shape[0]
    lf = _log_sigmoid(_mm(a_ref[...], w_ref[...]) + b_ref[...])
    lf_ref[...] = lf
    shift = int(math.log2(min(seg, tm)))
    r = lax.broadcasted_iota(I32, (tm, tm), 0)
    c = lax.broadcasted_iota(I32, (tm, tm), 1)
    same = lax.shift_right_logical(r, shift) == lax.shift_right_logical(c, shift)
    lower = jnp.where(same & (c <= r), 1.0, 0.0).astype(BF16)
    upper = jnp.where(same & (c > r), 1.0, 0.0).astype(BF16)
    pre = _dot_sel(lower, lf)
    suf_ref[...] = _dot_sel(upper, lf)
    if seg > tm:
        per = seg // tm

        @pl.when(i % per == 0)
        def _():
            carry_ref[...] = jnp.zeros_like(carry_ref)

        pre = pre + carry_ref[...]
        carry_ref[...] = pre[tm - 1:tm, :]
    ck_ref[...] = pre


def _logf_call(h, w_fg, b_f, seg, tm):
    m, k = h.shape
    nh = w_fg.shape[1]
    w_pad = jnp.pad(w_fg, ((0, 0), (0, LANE - nh)))
    b_pad = jnp.pad(b_f.reshape(1, nh), ((0, 0), (0, LANE - nh)))
    spec = pl.BlockSpec((tm, LANE), lambda i: (i, 0))
    shp = jax.ShapeDtypeStruct((m, LANE), F32)
    lf, ck, suf = pl.pallas_call(
        functools.partial(_logf_kernel, seg=seg),
        grid=(m // tm,),
        in_specs=[
            pl.BlockSpec((tm, k), lambda i: (i, 0)),
            pl.BlockSpec((k, LANE), lambda i: (0, 0)),
            pl.BlockSpec((1, LANE), lambda i: (0, 0)),
        ],
        out_specs=[spec, spec, spec],
        out_shape=[shp, shp, shp],
        scratch_shapes=[pltpu.VMEM((1, LANE), F32)],
        compiler_params=_cparams(("arbitrary",)),
        name="logf",
    )(h, w_pad, b_pad)
    return lf[:, :nh], ck[:, :nh], suf[:, :nh]


def _tri_pairs(n, descending):
    qi, kj = [], []
    for a in range(n):
        ks = range(a, -1, -1) if descending else range(a + 1)
        for b in ks:
            qi.append(a)
            kj.append(b)
    return jnp.asarray(qi, I32), jnp.asarray(kj, I32)


def _fox_prompt_kernel(qi_tab, kj_tab, q_ref, k_ref, v_ref, cq_ref, ck_ref, o_ref,
                       m_scr, l_scr, acc_scr, *, n_heads, scale):
    p = pl.program_id(1)
    qi, kj = qi_tab[p], kj_tab[p]
    tq, tk = q_ref.shape[0], k_ref.shape[0]

    @pl.when(kj == 0)
    def _():
        m_scr[...] = jnp.full_like(m_scr, NEG)
        l_scr[...] = jnp.zeros_like(l_scr)
        acc_scr[...] = jnp.zeros_like(acc_scr)

    def step(masked):
        if masked:
            causal = (lax.broadcasted_iota(I32, (tq, tk), 1)
                      <= lax.broadcasted_iota(I32, (tq, tk), 0))
        for h in range(n_heads):
            hs = slice(h * HEAD_DIM, (h + 1) * HEAD_DIM)
            s = _dot(q_ref[:, hs], k_ref[:, hs], NT) * scale
            s = s + (cq_ref[:, h:h + 1] - ck_ref[h:h + 1, :])
            if masked:
                s = jnp.where(causal, s, NEG)
            m_prev = m_scr[:, h:h + 1]
            m_new = jnp.maximum(m_prev, jnp.max(s, axis=-1, keepdims=True))
            alpha = jnp.exp(m_prev - m_new)
            pr = jnp.exp(s - m_new)
            l_scr[:, h:h + 1] = alpha * l_scr[:, h:h + 1] + jnp.sum(pr, axis=-1, keepdims=True)
            acc_scr[:, hs] = alpha * acc_scr[:, hs] + _dot(pr.astype(BF16), v_ref[:, hs])
            m_scr[:, h:h + 1] = m_new

    @pl.when(kj < qi)
    def _():
        step(False)

    @pl.when(kj == qi)
    def _():
        step(True)
        for h in range(n_heads):
            hs = slice(h * HEAD_DIM, (h + 1) * HEAD_DIM)
            o_ref[:, hs] = (acc_scr[:, hs] / l_scr[:, h:h + 1]).astype(o_ref.dtype)


def _fox_prompt_call(qkv, cq, ck_rows, batch, seq, n_heads, tq):
    w = n_heads * HEAD_DIM
    nq = seq // tq
    qi_tab, kj_tab = _tri_pairs(nq, descending=False)
    grid_spec = pltpu.PrefetchScalarGridSpec(
        num_scalar_prefetch=2,
        grid=(batch, int(qi_tab.shape[0])),
        in_specs=[
            pl.BlockSpec((tq, w), lambda b, p, qt, kt: (b * nq + qt[p], 0)),
            pl.BlockSpec((tq, w), lambda b, p, qt, kt: (b * nq + kt[p], 1)),
            pl.BlockSpec((tq, w), lambda b, p, qt, kt: (b * nq + kt[p], 2)),
            pl.BlockSpec((tq, n_heads), lambda b, p, qt, kt: (b * nq + qt[p], 0)),
            pl.BlockSpec((None, n_heads, tq), lambda b, p, qt, kt: (b, 0, kt[p])),
        ],
        out_specs=pl.BlockSpec((tq, w), lambda b, p, qt, kt: (b * nq + qt[p], 0)),
        scratch_shapes=[pltpu.VMEM((tq, n_heads), F32), pltpu.VMEM((tq, n_heads), F32),
                        pltpu.VMEM((tq, w), F32)],
    )
    return pl.pallas_call(
        functools.partial(_fox_prompt_kernel, n_heads=n_heads, scale=HEAD_DIM ** -0.5),
        grid_spec=grid_spec,
        out_shape=jax.ShapeDtypeStruct((batch * seq, w), BF16),
        compiler_params=_cparams(("arbitrary", "arbitrary")),
        name="fox_prompt",
    )(qi_tab, kj_tab, qkv, qkv, qkv, cq, ck_rows)


def _t5_bucket(n):
    n = jnp.maximum(n, 0)
    exact = N_BUCKETS // 2
    nf = jnp.maximum(n, 1).astype(F32)
    large = exact + (jnp.log(nf / exact) / math.log(MAX_DISTANCE / exact)
                     * (N_BUCKETS - exact)).astype(I32)
    return jnp.where(n < exact, n, jnp.minimum(large, N_BUCKETS - 1))


def _bias_tile_kernel(rb_ref, o_ref, *, t):
    i = lax.broadcasted_iota(I32, (t, t), 0)
    j = lax.broadcasted_iota(I32, (t, t), 1)
    d = i - j
    bucket = _t5_bucket(jnp.where(d < 0, d + t, d))
    h = pl.program_id(0)
    acc = jnp.zeros((t, t), F32)
    for b in range(N_BUCKETS):
        acc = jnp.where(bucket == b, rb_ref[b, h], acc)
    o_ref[...] = acc


def _bias_tile_call(rel_bias, t):
    n_heads = rel_bias.shape[1]
    return pl.pallas_call(
        functools.partial(_bias_tile_kernel, t=t),
        grid=(n_heads,),
        in_specs=[pl.BlockSpec(memory_space=pltpu.SMEM)],
        out_specs=pl.BlockSpec((None, t, t), lambda h: (h, 0, 0)),
        out_shape=jax.ShapeDtypeStruct((n_heads, t, t), F32),
        compiler_params=_cparams(("arbitrary",)),
        name="bias_tile",
    )(rel_bias)


def _diff_lambda(lam_ref, lam_init):
    lp = lam_ref[...]
    a = jnp.sum(lp[0:1, :] * lp[1:2, :], axis=-1, keepdims=True)
    b = jnp.sum(lp[2:3, :] * lp[3:4, :], axis=-1, keepdims=True)
    return jnp.exp(a) - jnp.exp(b) + lam_init


def _diff_prompt_kernel(qi_tab, kj_tab, q_ref, k_ref, v_ref, tt_ref, lam_ref, o_ref,
                        m_scr, l_scr, acc_scr, *, n_heads, scale, lam_init):
    p = pl.program_id(1)
    qi, kj = qi_tab[p], kj_tab[p]
    tq, tk = q_ref.shape[0], k_ref.shape[0]

    @pl.when(kj == 0)
    def _():
        m_scr[...] = jnp.full_like(m_scr, NEG)
        l_scr[...] = jnp.zeros_like(l_scr)
        acc_scr[...] = jnp.zeros_like(acc_scr)

    def step(diag):
        row = lax.broadcasted_iota(I32, (tq, tk), 0)
        col = lax.broadcasted_iota(I32, (tq, tk), 1)
        lane = lax.broadcasted_iota(I32, (1, HEAD_DIM), 1)
        first_half = lane < DIFF_DK
        if diag:
            causal = col <= row
        else:
            near = col > row + jnp.where(kj == qi - 1, 0, tk)
        for h in range(n_heads):
            hs = slice(h * HEAD_DIM, (h + 1) * HEAD_DIM)
            q = q_ref[:, hs]
            zero = jnp.zeros_like(q)
            q2 = jnp.concatenate([jnp.where(first_half, q, zero),
                                  jnp.where(first_half, zero, q)], axis=0)
            s = _dot(q2, k_ref[:, hs], NT) * scale
            tt = tt_ref[h]
            if diag:
                bias = tt
            else:
                bias = jnp.where(near, tt, tt_ref[h, tq - 1:tq, 0:1])
            for c in range(2):
                rs = slice(c * tq, (c + 1) * tq)
                idx = 2 * h + c
                sc = s[rs] + bias
                if diag:
                    sc = jnp.where(causal, sc, NEG)
                m_prev = m_scr[:, idx:idx + 1]
                m_new = jnp.maximum(m_prev, jnp.max(sc, axis=-1, keepdims=True))
                alpha = jnp.exp(m_prev - m_new)
                pr = jnp.exp(sc - m_new)
                l_scr[:, idx:idx + 1] = (alpha * l_scr[:, idx:idx + 1]
                                         + jnp.sum(pr, axis=-1, keepdims=True))
                acc_scr[c, :, hs] = alpha * acc_scr[c, :, hs] + _dot(pr.astype(BF16), v_ref[:, hs])
                m_scr[:, idx:idx + 1] = m_new

    @pl.when(kj < qi)
    def _():
        step(False)

    @pl.when(kj == qi)
    def _():
        step(True)
        lam = _diff_lambda(lam_ref, lam_init)
        for h in range(n_heads):
            hs = slice(h * HEAD_DIM, (h + 1) * HEAD_DIM)
            o0 = acc_scr[0, :, hs] / l_scr[:, 2 * h:2 * h + 1]
            o1 = acc_scr[1, :, hs] / l_scr[:, 2 * h + 1:2 * h + 2]
            o = o0 - lam * o1
            o = o * lax.rsqrt(jnp.mean(o * o, axis=-1, keepdims=True) + EPS) * (1.0 - lam_init)
            o_ref[:, hs] = o.astype(o_ref.dtype)


def _diff_prompt_call(qkv, tt, lam_diff, batch, seq, n_heads, tq, lam_init):
    w = n_heads * HEAD_DIM
    nq = seq // tq
    qi_tab, kj_tab = _tri_pairs(nq, descending=False)
    grid_spec = pltpu.PrefetchScalarGridSpec(
        num_scalar_prefetch=2,
        grid=(batch, int(qi_tab.shape[0])),
        in_specs=[
            pl.BlockSpec((tq, w), lambda b, p, qt, kt: (b * nq + qt[p], 0)),
            pl.BlockSpec((tq, w), lambda b, p, qt, kt: (b * nq + kt[p], 1)),
            pl.BlockSpec((tq, w), lambda b, p, qt, kt: (b * nq + kt[p], 2)),
            pl.BlockSpec((n_heads, tq, tq), lambda b, p, qt, kt: (0, 0, 0)),
            pl.BlockSpec(lam_diff.shape, lambda b, p, qt, kt: (0, 0)),
        ],
        out_specs=pl.BlockSpec((tq, w), lambda b, p, qt, kt: (b * nq + qt[p], 0)),
        scratch_shapes=[pltpu.VMEM((tq, 2 * n_heads), F32), pltpu.VMEM((tq, 2 * n_heads), F32),
                        pltpu.VMEM((2, tq, w), F32)],
    )
    return pl.pallas_call(
        functools.partial(_diff_prompt_kernel, n_heads=n_heads, scale=DIFF_DK ** -0.5,
                          lam_init=lam_init),
        grid_spec=grid_spec,
        out_shape=jax.ShapeDtypeStruct((batch * seq, w), BF16),
        compiler_params=_cparams(("arbitrary", "arbitrary")),
        name="diff_prompt",
    )(qi_tab, kj_tab, qkv, qkv, qkv, tt, lam_diff)


def _suffix_matrix():
    l = lax.broadcasted_iota(I32, (LANE, 2 * LANE), 0)
    j = lax.broadcasted_iota(I32, (LANE, 2 * LANE), 1)
    return jnp.where((l > j) | (j >= LANE), 1.0, 0.0).astype(BF16)


def _sb_tile(z, valid, v, r, u2):
    tk = z.shape[1]
    chunks = [None] * (tk // LANE)
    for c in reversed(range(tk // LANE)):
        zc = z[:, c * LANE:(c + 1) * LANE]
        ls = _log_sigmoid(zc)
        lr = ls - zc
        if valid is not None:
            vc = valid[:, c * LANE:(c + 1) * LANE]
            lr = jnp.where(vc, lr, 0.0)
        hi, lo = _split2(lr)
        s2 = _dot(hi, u2) + _dot(lo, u2)
        a = jnp.exp(ls + (s2[:, :LANE] + r))
        if valid is not None:
            a = jnp.where(vc, a, 0.0)
        chunks[c] = a.astype(BF16)
        r = r + s2[:, LANE:LANE + 1]
    a_full = jnp.concatenate(chunks, axis=-1) if len(chunks) > 1 else chunks[0]
    return _dot(a_full, v), r


def _sb_prompt_kernel(qi_tab, kj_tab, q_ref, k_ref, v_ref, o_ref, r_scr, acc_scr, *,
                      n_heads, scale):
    p = pl.program_id(2)
    qi, kj = qi_tab[p], kj_tab[p]
    tq, tk = q_ref.shape[0], k_ref.shape[0]
    u2 = _suffix_matrix()

    def step(diag):
        valid = None
        if diag:
            valid = (lax.broadcasted_iota(I32, (tq, tk), 1)
                     < lax.broadcasted_iota(I32, (tq, tk), 0))
        for h in range(n_heads):
            hs = slice(h * HEAD_DIM, (h + 1) * HEAD_DIM)
            z = _dot(q_ref[:, hs], k_ref[:, hs], NT) * scale
            if diag:
                r0 = jnp.zeros((tq, 1), F32)
                pv, r1 = _sb_tile(z, valid, v_ref[:, hs], r0, u2)
                acc_scr[:, hs] = pv
            else:
                pv, r1 = _sb_tile(z, None, v_ref[:, hs], r_scr[:, h:h + 1], u2)
                acc_scr[:, hs] = acc_scr[:, hs] + pv
            r_scr[:, h:h + 1] = r1

    @pl.when(kj == qi)
    def _():
        step(True)

    @pl.when(kj < qi)
    def _():
        step(False)

    @pl.when(kj == 0)
    def _():
        o_ref[...] = acc_scr[...].astype(o_ref.dtype)


def _sb_prompt_call(qkv, batch, seq, n_heads, tq, heads_per_step):
    n_groups = n_heads // heads_per_step
    w = heads_per_step * HEAD_DIM
    nq = seq // tq
    qi_tab, kj_tab = _tri_pairs(nq, descending=True)
    grid_spec = pltpu.PrefetchScalarGridSpec(
        num_scalar_prefetch=2,
        grid=(batch, n_groups, int(qi_tab.shape[0])),
        in_specs=[
            pl.BlockSpec((tq, w), lambda b, g, p, qt, kt: (b * nq + qt[p], g)),
            pl.BlockSpec((tq, w), lambda b, g, p, qt, kt: (b * nq + kt[p], n_groups + g)),
            pl.BlockSpec((tq, w), lambda b, g, p, qt, kt: (b * nq + kt[p], 2 * n_groups + g)),
        ],
        out_specs=pl.BlockSpec((tq, w), lambda b, g, p, qt, kt: (b * nq + qt[p], g)),
        scratch_shapes=[pltpu.VMEM((tq, heads_per_step), F32), pltpu.VMEM((tq, w), F32)],
    )
    return pl.pallas_call(
        functools.partial(_sb_prompt_kernel, n_heads=heads_per_step, scale=HEAD_DIM ** -0.5),
        grid_spec=grid_spec,
        out_shape=jax.ShapeDtypeStruct((batch * seq, n_heads * HEAD_DIM), BF16),
        compiler_params=_cparams(("arbitrary", "arbitrary", "arbitrary")),
        name="sb_prompt",
    )(qi_tab, kj_tab, qkv, qkv, qkv)


def _outproj_kernel(*refs, n_parts):
    a_refs = refs[:n_parts]
    w_refs = refs[n_parts:2 * n_parts]
    x_ref, g_ref, o_ref = refs[2 * n_parts:]
    y = None
    for a_ref, w_ref in zip(a_refs, w_refs):
        t = _mm(a_ref[...], w_ref[...])
        y = t if y is None else y + t
    o_ref[...] = x_ref[...] + g_ref[...] * y


def _gate_spec(g, rows_per_batch, tm, tn, col_tiled):
    d = g.shape[1]
    col = (lambda j: j) if col_tiled else (lambda j: 0)
    if rows_per_batch % tm == 0:
        per = rows_per_batch // tm
        return g.reshape(-1, 1, d), pl.BlockSpec((None, 1, tn), lambda i, j: (i // per, 0, col(j)))
    return (jnp.repeat(g, rows_per_batch, axis=0),
            pl.BlockSpec((tm, tn), lambda i, j: (i, col(j))))


def _outproj_call(parts, w, x, g, rows_per_batch, tm, tn=512):
    m, d = x.shape
    kp = parts[0].shape[1]
    n_parts = len(parts)
    g_in, g_spec = _gate_spec(g, rows_per_batch, tm, tn, True)
    in_specs = [pl.BlockSpec((tm, kp), lambda i, j: (i, 0)) for _ in parts]
    in_specs += [pl.BlockSpec((kp, tn), functools.partial(lambda i, j, c: (c, j), c=c))
                 for c in range(n_parts)]
    in_specs += [pl.BlockSpec((tm, tn), lambda i, j: (i, j)), g_spec]
    return pl.pallas_call(
        functools.partial(_outproj_kernel, n_parts=n_parts),
        grid=(m // tm, d // tn),
        in_specs=in_specs,
        out_specs=pl.BlockSpec((tm, tn), lambda i, j: (i, j)),
        out_shape=jax.ShapeDtypeStruct((m, d), F32),
        compiler_params=_cparams(("arbitrary", "arbitrary")),
        name="outproj",
    )(*parts, *([w] * n_parts), x, g_in)


def _swiglu_kernel(h_ref, wg_ref, wu_ref, wd_ref, x_ref, g_ref, o_ref):
    j = pl.program_id(1)
    h = h_ref[...]
    gate = _mm(h, wg_ref[...])
    up = _mm(h, wu_ref[...])
    part = _mm(_silu(gate) * up, wd_ref[...])

    @pl.when(j == 0)
    def _():
        o_ref[...] = part

    @pl.when(j > 0)
    def _():
        o_ref[...] = o_ref[...] + part

    @pl.when(j == pl.num_programs(1) - 1)
    def _():
        o_ref[...] = x_ref[...] + g_ref[...] * o_ref[...]


def _swiglu_call(h, wg, wu, wd, x, g, rows_per_batch, tm, tf=256):
    m, d = x.shape
    f = wg.shape[1]
    g_in, g_spec = _gate_spec(g, rows_per_batch, tm, d, False)
    return pl.pallas_call(
        _swiglu_kernel,
        grid=(m // tm, f // tf),
        in_specs=[
            pl.BlockSpec((tm, d), lambda i, j: (i, 0)),
            pl.BlockSpec((d, tf), lambda i, j: (0, j)),
            pl.BlockSpec((d, tf), lambda i, j: (0, j)),
            pl.BlockSpec((tf, d), lambda i, j: (j, 0)),
            pl.BlockSpec((tm, d), lambda i, j: (i, 0), pipeline_mode=pl.Buffered(1)),
            g_spec,
        ],
        out_specs=pl.BlockSpec((tm, d), lambda i, j: (i, 0)),
        out_shape=jax.ShapeDtypeStruct((m, d), F32),
        compiler_params=_cparams(("arbitrary", "arbitrary")),
        name="swiglu",
    )(h, wg, wu, wd, x, g_in)


def _router_kernel(h_ref, w_ref, b_ref, id_ref, wt_ref):
    logits = _mm(h_ref[...], w_ref[...]) + b_ref[...]
    lane = lax.broadcasted_iota(I32, logits.shape, 1)
    logits = jnp.where(lane < N_EXPERTS, logits, -jnp.inf)
    m1 = jnp.max(logits, axis=-1, keepdims=True)
    i1 = jnp.min(jnp.where(logits == m1, lane, LANE), axis=-1, keepdims=True)
    rest = jnp.where(lane == i1, -jnp.inf, logits)
    m2 = jnp.max(rest, axis=-1, keepdims=True)
    i2 = jnp.min(jnp.where(rest == m2, lane, LANE), axis=-1, keepdims=True)
    e = jnp.exp(m2 - m1)
    w1 = 1.0 / (1.0 + e)
    w2 = e / (1.0 + e)
    id_ref[...] = jnp.where(lane == 0, i1, jnp.where(lane == 1, i2, 0))
    wt_ref[...] = jnp.where(lane == 0, w1, jnp.where(lane == 1, w2, 0.0))


def _router_call(h, w_router, b_router, tm):
    m, d = h.shape
    ne = w_router.shape[1]
    w_pad = jnp.pad(w_router, ((0, 0), (0, LANE - ne)))
    b_pad = jnp.pad(b_router.reshape(1, ne), ((0, 0), (0, LANE - ne)))
    spec = pl.BlockSpec((tm, LANE), lambda i: (i, 0))
    ids, wts = pl.pallas_call(
        _router_kernel,
        grid=(m // tm,),
        in_specs=[
            pl.BlockSpec((tm, d), lambda i: (i, 0)),
            pl.BlockSpec((d, LANE), lambda i: (0, 0)),
            pl.BlockSpec((1, LANE), lambda i: (0, 0)),
        ],
        out_specs=[spec, spec],
        out_shape=[jax.ShapeDtypeStruct((m, LANE), I32), jax.ShapeDtypeStruct((m, LANE), F32)],
        compiler_params=_cparams(("arbitrary",)),
        name="router",
    )(h, w_pad, b_pad)
    return ids[:, :2], wts[:, :2]


def _rank_kernel(id_ref, rank_ref, cnt_ref, carry_ref):
    i = pl.program_id(0)
    tm = id_ref.shape[0]

    @pl.when(i == 0)
    def _():
        carry_ref[...] = jnp.zeros_like(carry_ref)

    ids = id_ref[...]
    lane = lax.broadcasted_iota(I32, ids.shape, 1)
    oh0 = jnp.where(lane == ids[:, 0:1], 1.0, 0.0)
    oh1 = jnp.where(lane == ids[:, 1:2], 1.0, 0.0)
    sel = oh0 + oh1
    r = lax.broadcasted_iota(I32, (tm, tm), 0)
    c = lax.broadcasted_iota(I32, (tm, tm), 1)
    strict_lower = jnp.where(c < r, 1.0, 0.0).astype(BF16)
    pre = _dot(strict_lower, sel.astype(BF16)) + carry_ref[...]
    rank0 = jnp.sum(oh0 * pre, axis=-1, keepdims=True)
    rank1 = jnp.sum(oh1 * pre, axis=-1, keepdims=True)
    rank_ref[...] = jnp.where(lane == 0, rank0, jnp.where(lane == 1, rank1, 0.0)).astype(I32)
    total = pre[tm - 1:tm, :] + sel[tm - 1:tm, :]
    carry_ref[...] = total
    cnt_ref[...] = total


def _rank_call(ids, tm=128):
    t = ids.shape[0]
    ids_pad = jnp.pad(ids, ((0, 0), (0, LANE - 2)), constant_values=-1)
    rank, cnt = pl.pallas_call(
        _rank_kernel,
        grid=(t // tm,),
        in_specs=[pl.BlockSpec((tm, LANE), lambda i: (i, 0))],
        out_specs=[pl.BlockSpec((tm, LANE), lambda i: (i, 0)),
                   pl.BlockSpec((1, LANE), lambda i: (0, 0))],
        out_shape=[jax.ShapeDtypeStruct((t, LANE), I32), jax.ShapeDtypeStruct((1, LANE), F32)],
        scratch_shapes=[pltpu.VMEM((1, LANE), F32)],
        compiler_params=_cparams(("arbitrary",)),
        name="rank",
    )(ids_pad)
    return rank[:, :2], cnt[0, :N_EXPERTS].astype(I32)


def _row_copy(src_ref, src_row, dst_ref, dst_row, sem):
    return pltpu.make_async_copy(src_ref.at[pl.ds(src_row, 1)], dst_ref.at[pl.ds(dst_row, 1)], sem)


def _dispatch_kernel(dest_ref, h_ref, hs_in_ref, hs_ref, sem):
    del hs_in_ref
    n = h_ref.shape[0]

    def start(r, carry):
        _row_copy(h_ref, r, hs_ref, dest_ref[0, 0, 2 * r], sem).start()
        _row_copy(h_ref, r, hs_ref, dest_ref[0, 0, 2 * r + 1], sem).start()
        return carry

    lax.fori_loop(0, n, start, 0)

    def wait(r, carry):
        _row_copy(h_ref, 0, hs_ref, 0, sem).wait()
        _row_copy(h_ref, 0, hs_ref, 0, sem).wait()
        return carry

    lax.fori_loop(0, n, wait, 0)


def _dispatch_call(h, dest, hs, tm):
    t, d = h.shape
    dest3 = dest.reshape(t // tm, 1, 2 * tm)
    return pl.pallas_call(
        _dispatch_kernel,
        grid=(t // tm,),
        in_specs=[
            pl.BlockSpec((1, 1, 2 * tm), lambda i: (i, 0, 0), memory_space=pltpu.SMEM),
            pl.BlockSpec((tm, d), lambda i: (i, 0)),
            pl.BlockSpec(memory_space=pl.ANY),
        ],
        out_specs=pl.BlockSpec(memory_space=pl.ANY),
        out_shape=jax.ShapeDtypeStruct(hs.shape, hs.dtype),
        scratch_shapes=[pltpu.SemaphoreType.DMA],
        input_output_aliases={2: 0},
        compiler_params=_cparams(("arbitrary",)),
        name="dispatch",
    )(dest3, h, hs)


def _grouped_kernel(te_tab, nv_tab, hs_ref, wg_ref, wu_ref, wd_ref, ys_ref, x_scr):
    i, j = pl.program_id(0), pl.program_id(1)

    @pl.when(i < nv_tab[0])
    def _():
        @pl.when(j == 0)
        def _():
            x_scr[...] = hs_ref[...].astype(BF16)

        x = x_scr[...]
        gate = _dot(x, wg_ref[...].astype(BF16))
        up = _dot(x, wu_ref[...].astype(BF16))
        part = _dot((_silu(gate) * up).astype(BF16), wd_ref[...].astype(BF16))

        @pl.when(j == 0)
        def _():
            ys_ref[...] = part

        @pl.when(j > 0)
        def _():
            ys_ref[...] = ys_ref[...] + part

    @pl.when(jnp.logical_and(i >= nv_tab[0], j == 0))
    def _():
        ys_ref[...] = jnp.zeros_like(ys_ref)


def _grouped_call(hs, tile_expert, n_valid, wg, wu, wd, tm, tf=256):
    r, d = hs.shape
    f = wg.shape[2]
    n_tiles, nf = r // tm, f // tf

    def row_map(i, j, te, nv):
        return (jnp.minimum(i, nv[0] - 1), 0)

    def col_blk(i, j, nv):
        return jnp.where(i < nv[0], j, nf - 1)

    grid_spec = pltpu.PrefetchScalarGridSpec(
        num_scalar_prefetch=2,
        grid=(n_tiles, nf),
        in_specs=[
            pl.BlockSpec((tm, d), row_map),
            pl.BlockSpec((None, d, tf), lambda i, j, te, nv: (te[i], 0, col_blk(i, j, nv))),
            pl.BlockSpec((None, d, tf), lambda i, j, te, nv: (te[i], 0, col_blk(i, j, nv))),
            pl.BlockSpec((None, tf, d), lambda i, j, te, nv: (te[i], col_blk(i, j, nv), 0)),
        ],
        out_specs=pl.BlockSpec((tm, d), lambda i, j, te, nv: (i, 0)),
        scratch_shapes=[pltpu.VMEM((tm, d), BF16)],
    )
    return pl.pallas_call(
        _grouped_kernel,
        grid_spec=grid_spec,
        out_shape=jax.ShapeDtypeStruct((r, d), F32),
        compiler_params=_cparams(("arbitrary", "arbitrary")),
        name="grouped_ffn",
    )(tile_expert, n_valid, hs, wg, wu, wd)


def _combine_kernel(dest_ref, wt_ref, x_ref, g_ref, ys_ref, o_ref, buf, sem):
    n = x_ref.shape[0]

    def start(r, carry):
        _row_copy(ys_ref, dest_ref[0, 0, 2 * r], buf.at[0], r, sem).start()
        _row_copy(ys_ref, dest_ref[0, 0, 2 * r + 1], buf.at[1], r, sem).start()
        return carry

    lax.fori_loop(0, n, start, 0)

    def wait(r, carry):
        _row_copy(ys_ref, 0, buf.at[0], 0, sem).wait()
        _row_copy(ys_ref, 0, buf.at[1], 0, sem).wait()
        return carry

    lax.fori_loop(0, n, wait, 0)
    wt = wt_ref[...]
    f = wt[:, 0:1] * buf[0] + wt[:, 1:2] * buf[1]
    o_ref[...] = x_ref[...] + g_ref[...] * f


def _combine_call(ys, dest, wts, x, g, rows_per_batch, tm):
    t, d = x.shape
    dest3 = dest.reshape(t // tm, 1, 2 * tm)
    wt_pad = jnp.pad(wts, ((0, 0), (0, LANE - 2)))
    if rows_per_batch % tm == 0:
        per = rows_per_batch // tm
        g_in, g_spec = g.reshape(-1, 1, d), pl.BlockSpec((None, 1, d), lambda i: (i // per, 0, 0))
    else:
        g_in, g_spec = jnp.repeat(g, rows_per_batch, axis=0), pl.BlockSpec((tm, d), lambda i: (i, 0))
    return pl.pallas_call(
        _combine_kernel,
        grid=(t // tm,),
        in_specs=[
            pl.BlockSpec((1, 1, 2 * tm), lambda i: (i, 0, 0), memory_space=pltpu.SMEM),
            pl.BlockSpec((tm, LANE), lambda i: (i, 0)),
            pl.BlockSpec((tm, d), lambda i: (i, 0)),
            g_spec,
            pl.BlockSpec(memory_space=pl.ANY),
        ],
        out_specs=pl.BlockSpec((tm, d), lambda i: (i, 0)),
        out_shape=jax.ShapeDtypeStruct((t, d), F32),
        scratch_shapes=[pltpu.VMEM((2, tm, d), F32), pltpu.SemaphoreType.DMA],
        compiler_params=_cparams(("arbitrary",)),
        name="combine",
    )(dest3, wt_pad, x, g_in, ys)


def _moe(h_groups, x_groups, g_groups, rows_per_batch, w_router, b_router, wg, wu, wd,
         tm_tok, tm_grp):
    d = h_groups[0].shape[1]
    routed = [_router_call(h, w_router, b_router, tm_tok) for h in h_groups]
    ids = jnp.concatenate([r[0] for r in routed], axis=0)
    n_tok = ids.shape[0]
    rank, counts = _rank_call(ids)
    padded = (counts + tm_grp - 1) // tm_grp * tm_grp
    ends = jnp.cumsum(padded)
    starts = ends - padded
    dest = starts[ids] + rank
    n_tiles = -(-2 * n_tok // tm_grp) + N_EXPERTS
    tile_row = jnp.arange(n_tiles, dtype=I32) * tm_grp
    tile_expert = jnp.minimum(jnp.sum(tile_row[:, None] >= ends[None, :], axis=1),
                              N_EXPERTS - 1).astype(I32)
    n_valid = (ends[-1] // tm_grp).astype(I32).reshape(1)
    tile_expert = jnp.where(tile_row < ends[-1], tile_expert,
                            tile_expert[jnp.maximum(n_valid[0] - 1, 0)])
    hs = jnp.zeros((n_tiles * tm_grp, d), F32)
    off = 0
    for h in h_groups:
        n = h.shape[0]
        hs = _dispatch_call(h, dest[off:off + n], hs, tm_tok)
        off += n
    ys = _grouped_call(hs, tile_expert, n_valid, wg, wu, wd, tm_grp)
    outs, off = [], 0
    for (ids_g, wts_g), x, g, rpb in zip(routed, x_groups, g_groups, rows_per_batch):
        n = x.shape[0]
        outs.append(_combine_call(ys, dest[off:off + n], wts_g, x, g, rpb, tm_tok))
        off += n
    return outs


PAGES_PER_STEP = 4


def _rows2d(ref3):
    k, h, d = ref3.shape
    return ref3[...].reshape(k * h, d)


def _new_rows(nkv_ref, which, n_rows):
    x = nkv_ref[:, which].reshape(n_rows, HEAD_DIM)
    return jnp.concatenate([x, jnp.zeros((LANE - n_rows, HEAD_DIM), F32)], axis=0)


def _head_match(n_rows, n_cols, n_heads, n_tok):
    r = lax.broadcasted_iota(I32, (n_rows, n_cols), 0)
    c = lax.broadcasted_iota(I32, (n_rows, n_cols), 1)
    hshift, tshift = int(math.log2(n_heads)), int(math.log2(n_tok))
    same = (lax.shift_right_logical(r, tshift) & (n_heads - 1)) == (c & (n_heads - 1))
    return same, lax.shift_right_logical(c, hshift), r & (n_tok - 1)


def _page_specs(n_pages, page, n_heads, which, n_steps, active_phase):
    def make(slot):
        def index(b, t, pt):
            a = t % n_steps
            if active_phase is not None:
                mine = (t // n_steps) == active_phase
                a = jnp.where(mine, a, 1 if active_phase == 1 else n_steps - 1)
            pos = n_pages - 1 - ((jnp.maximum(a, 1) - 1) * PAGES_PER_STEP + slot)
            return (0, pt[b, pos], 0, which, 0, 0)
        return pl.BlockSpec((None, None, page, None, n_heads, HEAD_DIM), index)
    return [make(s) for s in range(PAGES_PER_STEP)]


def _softmax_stats(s, m_scr, l_scr):
    m_prev = m_scr[...]
    m_new = jnp.maximum(m_prev, jnp.max(s, axis=-1, keepdims=True))
    l_scr[...] = (jnp.exp(m_prev - m_new) * l_scr[...]
                  + jnp.sum(jnp.exp(s - m_new), axis=-1, keepdims=True))
    m_scr[...] = m_new


def _fox_sample_kernel(pt_ref, q_ref, nkv_ref, nlf_ref, rowsuf_ref, *refs,
                       n_heads, n_tok, scale, n_steps):
    del pt_ref
    pps = PAGES_PER_STEP
    k_refs, v_refs, lf_refs = refs[:pps], refs[pps:2 * pps], refs[2 * pps:3 * pps]
    o_ref, s_scr, snew_scr, m_scr, l_scr, acc_scr, carry_scr = refs[3 * pps:]
    t = pl.program_id(1)
    n_rows = n_heads * n_tok
    hshift = int(math.log2(n_heads))
    q = q_ref[...].astype(BF16)

    def score(k2d, lfv, is_new):
        n_cols = k2d.shape[0]
        n_chunks = n_cols // LANE
        l1 = lax.broadcasted_iota(I32, (LANE, LANE), 0)
        l2 = lax.broadcasted_iota(I32, (LANE, LANE), 1)
        same_head = (l1 & (n_heads - 1)) == (l2 & (n_heads - 1))
        later_key = lax.shift_right_logical(l1, hshift) > lax.shift_right_logical(l2, hshift)
        within_m = jnp.where(same_head & later_key, 1.0, 0.0).astype(BF16)
        total_m = jnp.where(same_head, 1.0, 0.0).astype(BF16)
        within = _dot_sel(within_m, lfv, sel_first=False)
        rowtot = _dot_sel(total_m, lfv, sel_first=False)
        ridx = lax.broadcasted_iota(I32, lfv.shape, 0)
        rowsuf = jnp.zeros(lfv.shape, F32)
        for a in range(1, lfv.shape[0]):
            rowsuf = rowsuf + jnp.where(ridx < a, rowtot[a:a + 1, :], 0.0)
        suf = within + rowsuf + carry_scr[...]
        carry_scr[...] = carry_scr[...] + jnp.sum(rowtot, axis=0, keepdims=True)
        bias = (jnp.concatenate([suf[a:a + 1, :] for a in range(n_chunks)], axis=1)
                if n_chunks > 1 else suf[0:1, :])
        s = _dot(q, k2d.astype(BF16), NT) * scale + (bias - rowsuf_ref[...])
        same, key, tok = _head_match(n_rows, n_cols, n_heads, n_tok)
        valid = (same & (key <= tok)) if is_new else same
        return jnp.where(valid, s, NEG)

    def weights(s):
        return (jnp.exp(s - m_scr[...]) * (1.0 / l_scr[...])).astype(BF16)

    @pl.when(t == 0)
    def _():
        m_scr[...] = jnp.full_like(m_scr, NEG)
        l_scr[...] = jnp.zeros_like(l_scr)
        carry_scr[...] = jnp.zeros_like(carry_scr)
        s = score(_new_rows(nkv_ref, 0, n_rows), nlf_ref[...], True)
        snew_scr[...] = s
        _softmax_stats(s, m_scr, l_scr)

    @pl.when(jnp.logical_and(t > 0, t < n_steps))
    def _():
        for i in range(pps):
            s = score(_rows2d(k_refs[i]), lf_refs[i][...], False)
            s_scr[(t - 1) * pps + i] = s
            _softmax_stats(s, m_scr, l_scr)

    @pl.when(t == n_steps)
    def _():
        acc_scr[...] = _dot(weights(snew_scr[...]), _new_rows(nkv_ref, 1, n_rows).astype(BF16))

    @pl.when(t > n_steps)
    def _():
        for i in range(pps):
            pw = weights(s_scr[(t - n_steps - 1) * pps + i])
            acc_scr[...] = acc_scr[...] + _dot(pw, _rows2d(v_refs[i]).astype(BF16))

    @pl.when(t == 2 * n_steps - 1)
    def _():
        o_ref[...] = acc_scr[...]


def _sample_rows(x, batch, n_tok, n_heads):
    return (x.reshape(batch, n_tok, n_heads, HEAD_DIM).transpose(0, 2, 1, 3)
            .reshape(batch, n_heads * n_tok, HEAD_DIM))


def _sample_unrows(o, batch, n_tok, n_heads):
    return (o.reshape(batch, n_heads, n_tok, HEAD_DIM).transpose(0, 2, 1, 3)
            .reshape(batch * n_tok, n_heads * HEAD_DIM))


def _fox_sample_call(q, new_kv, new_lf, new_suf, cache_kv, cache_lf, page_table, n_tok):
    batch, n_pages = page_table.shape
    n_layers, n_phys, page, _, n_heads, _ = cache_kv.shape
    pps = PAGES_PER_STEP
    assert n_pages % pps == 0
    n_steps = 1 + n_pages // pps
    n_rows = n_heads * n_tok
    n_cols = page * n_heads
    lf_rows = n_cols // LANE
    q_rows = _sample_rows(q, batch, n_tok, n_heads)
    nkv = new_kv.reshape(batch, n_tok, 2, n_heads, HEAD_DIM)
    nlf = jnp.pad(new_lf.reshape(batch, 1, n_rows), ((0, 0), (0, 7), (0, LANE - n_rows)))
    rowsuf = new_suf.reshape(batch, n_tok, n_heads).transpose(0, 2, 1).reshape(batch, n_rows, 1)
    lf_pages = cache_lf.reshape(n_layers, n_phys, lf_rows, LANE)

    def lf_spec(slot):
        def index(b, t, pt):
            a = jnp.where(t < n_steps, t, n_steps - 1)
            pos = n_pages - 1 - ((jnp.maximum(a, 1) - 1) * pps + slot)
            return (0, pt[b, pos], 0, 0)
        return pl.BlockSpec((None, None, lf_rows, LANE), index)

    grid_spec = pltpu.PrefetchScalarGridSpec(
        num_scalar_prefetch=1,
        grid=(batch, 2 * n_steps),
        in_specs=[
            pl.BlockSpec((None, n_rows, HEAD_DIM), lambda b, t, pt: (b, 0, 0)),
            pl.BlockSpec((None, n_tok, 2, n_heads, HEAD_DIM), lambda b, t, pt: (b, 0, 0, 0, 0)),
            pl.BlockSpec((None, 8, LANE), lambda b, t, pt: (b, 0, 0)),
            pl.BlockSpec((None, n_rows, 1), lambda b, t, pt: (b, 0, 0)),
            *_page_specs(n_pages, page, n_heads, 0, n_steps, 0),
            *_page_specs(n_pages, page, n_heads, 1, n_steps, 1),
            *[lf_spec(s) for s in range(pps)],
        ],
        out_specs=pl.BlockSpec((None, n_rows, HEAD_DIM), lambda b, t, pt: (b, 0, 0)),
        scratch_shapes=[pltpu.VMEM((n_pages, n_rows, n_cols), F32),
                        pltpu.VMEM((n_rows, LANE), F32),
                        pltpu.VMEM((n_rows, 1), F32), pltpu.VMEM((n_rows, 1), F32),
                        pltpu.VMEM((n_rows, HEAD_DIM), F32), pltpu.VMEM((1, LANE), F32)],
    )
    o = pl.pallas_call(
        functools.partial(_fox_sample_kernel, n_heads=n_heads, n_tok=n_tok,
                          scale=HEAD_DIM ** -0.5, n_steps=n_steps),
        grid_spec=grid_spec,
        out_shape=jax.ShapeDtypeStruct((batch, n_rows, HEAD_DIM), F32),
        compiler_params=_cparams(("arbitrary", "arbitrary")),
        name="fox_sample",
    )(page_table, q_rows, nkv, nlf, rowsuf, *([cache_kv] * (2 * pps)), *([lf_pages] * pps))
    return _sample_unrows(o, batch, n_tok, n_heads)


def _diff_sample_kernel(pt_ref, rb_ref, q_ref, nkv_ref, lam_ref, *refs,
                        n_heads, n_tok, page, scale, lam_init, n_steps):
    del pt_ref
    pps = PAGES_PER_STEP
    k_refs, v_refs = refs[:pps], refs[pps:2 * pps]
    o_ref, s_scr, snew_scr, m_scr, l_scr, acc_scr = refs[2 * pps:]
    t = pl.program_id(1)
    n_rows = n_heads * n_tok
    n_cols = page * n_heads
    q2 = q_ref[...].astype(BF16)

    def rel_bias_cols():
        hh = lax.broadcasted_iota(I32, (LANE, n_cols), 0)
        cc = lax.broadcasted_iota(I32, (LANE, n_cols), 1)
        head_sel = jnp.where(hh == (cc & (n_heads - 1)), 1.0, 0.0).astype(BF16)
        return _dot_sel(head_sel, rb_ref[...], sel_first=False)

    def score(k2d, rb_cols, dist0, is_new):
        nc = k2d.shape[0]
        s = _dot(q2, k2d.astype(BF16), NT) * scale
        same, key, tok = _head_match(2 * n_rows, nc, n_heads, n_tok)
        if dist0 is None:
            bias = rb_cols[N_BUCKETS - 1:N_BUCKETS, :nc]
        else:
            bucket = _t5_bucket(dist0 + tok - key)
            bias = jnp.zeros(s.shape, F32)
            for b in range(N_BUCKETS):
                bias = jnp.where(bucket == b, rb_cols[b:b + 1, :nc], bias)
        valid = (same & (key <= tok)) if is_new else same
        return jnp.where(valid, s + bias, NEG)

    def weights(s):
        pn = jnp.exp(s - m_scr[...]) * (1.0 / l_scr[...])
        lam = _diff_lambda(lam_ref, lam_init)
        return (pn[:n_rows] - lam * pn[n_rows:]).astype(BF16)

    @pl.when(t == 0)
    def _():
        m_scr[...] = jnp.full_like(m_scr, NEG)
        l_scr[...] = jnp.zeros_like(l_scr)
        s = score(_new_rows(nkv_ref, 0, n_rows), rel_bias_cols(), 0, True)
        snew_scr[...] = s
        _softmax_stats(s, m_scr, l_scr)

    @pl.when(t == 1)
    def _():
        rb_cols = rel_bias_cols()
        for i in range(pps):
            s = score(_rows2d(k_refs[i]), rb_cols, page if i == 0 else None, False)
            s_scr[i] = s
            _softmax_stats(s, m_scr, l_scr)

    @pl.when(jnp.logical_and(t > 1, t < n_steps))
    def _():
        rb_cols = rel_bias_cols()
        for i in range(pps):
            s = score(_rows2d(k_refs[i]), rb_cols, None, False)
            s_scr[(t - 1) * pps + i] = s
            _softmax_stats(s, m_scr, l_scr)

    @pl.when(t == n_steps)
    def _():
        acc_scr[...] = _dot(weights(snew_scr[...]), _new_rows(nkv_ref, 1, n_rows).astype(BF16))

    @pl.when(t > n_steps)
    def _():
        for i in range(pps):
            pw = weights(s_scr[(t - n_steps - 1) * pps + i])
            acc_scr[...] = acc_scr[...] + _dot(pw, _rows2d(v_refs[i]).astype(BF16))

    @pl.when(t == 2 * n_steps - 1)
    def _():
        o = acc_scr[...]
        o_ref[...] = o * lax.rsqrt(jnp.mean(o * o, axis=-1, keepdims=True) + EPS) * (1.0 - lam_init)


def _diff_sample_call(q, new_kv, cache_kv, page_table, rel_bias, lam_diff, n_tok, lam_init):
    batch, n_pages = page_table.shape
    _, _, page, _, n_heads, _ = cache_kv.shape
    pps = PAGES_PER_STEP
    assert n_pages % pps == 0
    assert page >= MAX_DISTANCE
    n_steps = 1 + n_pages // pps
    n_rows = n_heads * n_tok
    n_cols = page * n_heads
    q_rows = _sample_rows(q, batch, n_tok, n_heads)
    lane = jnp.arange(HEAD_DIM) < DIFF_DK
    q2 = jnp.concatenate([jnp.where(lane, q_rows, 0.0), jnp.where(lane, 0.0, q_rows)], axis=1)
    nkv = new_kv.reshape(batch, n_tok, 2, n_heads, HEAD_DIM)
    rb_pad = jnp.pad(rel_bias, ((0, 0), (0, LANE - n_heads)))
    grid_spec = pltpu.PrefetchScalarGridSpec(
        num_scalar_prefetch=1,
        grid=(batch, 2 * n_steps),
        in_specs=[
            pl.BlockSpec(rb_pad.shape, lambda b, t, pt: (0, 0)),
            pl.BlockSpec((None, 2 * n_rows, HEAD_DIM), lambda b, t, pt: (b, 0, 0)),
            pl.BlockSpec((None, n_tok, 2, n_heads, HEAD_DIM), lambda b, t, pt: (b, 0, 0, 0, 0)),
            pl.BlockSpec(lam_diff.shape, lambda b, t, pt: (0, 0)),
            *_page_specs(n_pages, page, n_heads, 0, n_steps, 0),
            *_page_specs(n_pages, page, n_heads, 1, n_steps, 1),
        ],
        out_specs=pl.BlockSpec((None, n_rows, HEAD_DIM), lambda b, t, pt: (b, 0, 0)),
        scratch_shapes=[pltpu.VMEM((n_pages, 2 * n_rows, n_cols), F32),
                        pltpu.VMEM((2 * n_rows, LANE), F32),
                        pltpu.VMEM((2 * n_rows, 1), F32), pltpu.VMEM((2 * n_rows, 1), F32),
                        pltpu.VMEM((n_rows, HEAD_DIM), F32)],
    )
    o = pl.pallas_call(
        functools.partial(_diff_sample_kernel, n_heads=n_heads, n_tok=n_tok, page=page,
                          scale=DIFF_DK ** -0.5, lam_init=lam_init, n_steps=n_steps),
        grid_spec=grid_spec,
        out_shape=jax.ShapeDtypeStruct((batch, n_rows, HEAD_DIM), F32),
        compiler_params=_cparams(("arbitrary", "arbitrary")),
        name="diff_sample",
    )(page_table, rb_pad, q2, nkv, lam_diff, *([cache_kv] * (2 * pps)))
    return _sample_unrows(o, batch, n_tok, n_heads)


def _sb_sample_kernel(pt_ref, q_ref, nkv_ref, gat_ref, sct_ref, *refs, n_heads, n_tok, scale):
    del pt_ref
    pps = PAGES_PER_STEP
    k_refs, v_refs = refs[:pps], refs[pps:2 * pps]
    o_ref, acc_scr, r_scr = refs[2 * pps:]
    t = pl.program_id(1)
    n_rows = n_heads * n_tok
    q = q_ref[...].astype(BF16)

    def attend(k2d, v2d, is_new):
        nc = k2d.shape[0]
        a_ = lax.broadcasted_iota(I32, (LANE, LANE), 0)
        b_ = lax.broadcasted_iota(I32, (LANE, LANE), 1)
        later = jnp.where(a_ > b_, 1.0, 0.0).astype(BF16)
        z = _dot(q, k2d.astype(BF16), NT) * scale
        same, key, tok = _head_match(n_rows, nc, n_heads, n_tok)
        valid = (same & (key < tok)) if is_new else same
        ls = _log_sigmoid(z)
        lr = jnp.where(valid, ls - z, 0.0)
        compact = _dot_sel(gat_ref[:nc, :], lr, sel_first=False)
        suf = _dot_sel(later, compact, sel_first=False)
        after = _dot_sel(sct_ref[:, :nc], suf, sel_first=False) + r_scr[...]
        a = jnp.where(valid, jnp.exp(ls + after), 0.0)
        acc_scr[...] = acc_scr[...] + _dot(a.astype(BF16), v2d.astype(BF16))
        r_scr[...] = r_scr[...] + jnp.sum(compact, axis=-1, keepdims=True)

    @pl.when(t == 0)
    def _():
        acc_scr[...] = jnp.zeros_like(acc_scr)
        r_scr[...] = jnp.zeros_like(r_scr)
        attend(_new_rows(nkv_ref, 0, n_rows), _new_rows(nkv_ref, 1, n_rows), True)

    @pl.when(t > 0)
    def _():
        for i in range(pps):
            attend(_rows2d(k_refs[i]), _rows2d(v_refs[i]), False)

    @pl.when(t == pl.num_programs(1) - 1)
    def _():
        o_ref[...] = acc_scr[...]


def _sb_sample_call(q, new_kv, cache_kv, page_table, n_tok):
    batch, n_pages = page_table.shape
    _, _, page, _, n_heads, _ = cache_kv.shape
    pps = PAGES_PER_STEP
    assert n_pages % pps == 0 and page <= LANE
    n_steps = 1 + n_pages // pps
    n_rows = n_heads * n_tok
    n_cols = page * n_heads
    q_rows = _sample_rows(q, batch, n_tok, n_heads)
    nkv = new_kv.reshape(batch, n_tok, 2, n_heads, HEAD_DIM)
    key_of_col = jnp.arange(n_cols, dtype=I32) // n_heads
    gat = (key_of_col[:, None] == jnp.arange(LANE, dtype=I32)[None, :]).astype(BF16)
    grid_spec = pltpu.PrefetchScalarGridSpec(
        num_scalar_prefetch=1,
        grid=(batch, n_steps),
        in_specs=[
            pl.BlockSpec((None, n_rows, HEAD_DIM), lambda b, t, pt: (b, 0, 0)),
            pl.BlockSpec((None, n_tok, 2, n_heads, HEAD_DIM), lambda b, t, pt: (b, 0, 0, 0, 0)),
            pl.BlockSpec((n_cols, LANE), lambda b, t, pt: (0, 0)),
            pl.BlockSpec((LANE, n_cols), lambda b, t, pt: (0, 0)),
            *_page_specs(n_pages, page, n_heads, 0, n_steps, None),
            *_page_specs(n_pages, page, n_heads, 1, n_steps, None),
        ],
        out_specs=pl.BlockSpec((None, n_rows, HEAD_DIM), lambda b, t, pt: (b, 0, 0)),
        scratch_shapes=[pltpu.VMEM((n_rows, HEAD_DIM), F32), pltpu.VMEM((n_rows, 1), F32)],
    )
    o = pl.pallas_call(
        functools.partial(_sb_sample_kernel, n_heads=n_heads, n_tok=n_tok, scale=HEAD_DIM ** -0.5),
        grid_spec=grid_spec,
        out_shape=jax.ShapeDtypeStruct((batch, n_rows, HEAD_DIM), F32),
        compiler_params=_cparams(("arbitrary", "arbitrary")),
        name="sb_sample",
    )(page_table, q_rows, nkv, gat, gat.T, *([cache_kv] * (2 * pps)))
    return _sample_unrows(o, batch, n_tok, n_heads)


TM_PROMPT = 1024
TQ_PROMPT = 512
TM_TOKEN = 128
TM_GROUP = 512


def kernel(x_prompt, x_sample, cache_fox_kv, cache_fox_logf, cache_diff_kv, cache_sb_kv, page_table,
           c_prompt, c_sample, w_ada, b_ada, g_norm, w_in_even, b_forget, g_q_fox, g_k_fox,
           g_q_diff, g_k_diff, lam_diff, rel_bias, w_out_even, w_gate, w_up, w_down,
           w_in_odd, w_out_odd, w_router, b_router, w_gate_exp, w_up_exp, w_down_exp):
    bp, seq, d = x_prompt.shape
    bs, n_tok, _ = x_sample.shape
    h_fox = cache_fox_kv.shape[4]
    h_diff = cache_diff_kv.shape[4]
    h_sb = cache_sb_kv.shape[4]
    fox_w, diff_w, sb_w = h_fox * HEAD_DIM, h_diff * HEAD_DIM, h_sb * HEAD_DIM
    mp, ms = bp * seq, bs * n_tok

    n_c = bp + bs
    c_all = jnp.pad(jnp.concatenate([c_prompt, c_sample], axis=0), ((0, -n_c % 8), (0, 0)))
    mods = _ada_call(c_all, w_ada, b_ada)

    def mod6(layer, lo, hi):
        return [mods[layer, lo:hi, k * d:(k + 1) * d] for k in range(6)]

    xp = x_prompt.reshape(mp, d)
    xs = x_sample.reshape(ms, d)

    lam_init0 = 0.8 - 0.6 * math.exp(-0.3 * 0)
    w_in0 = w_in_even[0]
    w_fg = w_in0[:, 3 * fox_w:3 * fox_w + h_fox]
    w_diff_in = w_in0[:, 3 * fox_w + h_fox:]
    gain_fox = jnp.concatenate([jnp.tile(g_q_fox[0], h_fox), jnp.tile(g_k_fox[0], h_fox),
                                jnp.ones((fox_w,), F32)]).reshape(1, 3 * fox_w)
    gain_diff = jnp.concatenate([jnp.tile(g_q_diff[0], 2 * h_diff), jnp.tile(g_k_diff[0], 2 * h_diff),
                                 jnp.ones((diff_w,), F32)]).reshape(1, 3 * diff_w)
    tt = _bias_tile_call(rel_bias, TQ_PROMPT)

    sh1, sc1, g1, sh2, sc2, g2 = mod6(0, 0, bp)
    hp = _normmod_call(xp, g_norm[0, 0], sc1, sh1, BF16, seq, TM_PROMPT)
    fox_qkv, fox_kv_p = _inproj_call(hp, w_in0, 0, 3 * fox_w, gain_fox, HEAD_DIM, True, TM_PROMPT)
    diff_qkv, diff_kv_p = _inproj_call(hp, w_diff_in, 0, 3 * diff_w, gain_diff, DIFF_DK, True,
                                       TM_PROMPT)
    logf_p, ck_p, _ = _logf_call(hp, w_fg, b_forget[0], seq, TQ_PROMPT)
    ck_rows = ck_p.reshape(bp, seq, h_fox).transpose(0, 2, 1)
    o_fox = _fox_prompt_call(fox_qkv, ck_p, ck_rows, bp, seq, h_fox, TQ_PROMPT)
    o_diff = _diff_prompt_call(diff_qkv, tt, lam_diff[0], bp, seq, h_diff, TQ_PROMPT, lam_init0)
    xp = _outproj_call([o_fox, o_diff], w_out_even[0], xp, g1, seq, TM_PROMPT)
    hp = _normmod_call(xp, g_norm[0, 1], sc2, sh2, BF16, seq, TM_PROMPT)
    xp = _swiglu_call(hp, w_gate[0], w_up[0], w_down[0], xp, g2, seq, TM_PROMPT)

    sh1, sc1, g1, sh2, sc2, g2 = mod6(0, bp, bp + bs)
    hs_ = _normmod_call(xs, g_norm[0, 0], sc1, sh1, F32, n_tok, ms)
    fox_s = _inproj_call(hs_, w_in0, 0, 3 * fox_w, gain_fox, HEAD_DIM, False, ms)
    diff_s = _inproj_call(hs_, w_diff_in, 0, 3 * diff_w, gain_diff, DIFF_DK, False, ms)
    logf_s, _, suf_s = _logf_call(hs_, w_fg, b_forget[0], n_tok, ms)
    fox_kv_s, diff_kv_s = fox_s[:, fox_w:], diff_s[:, diff_w:]
    o_fox = _fox_sample_call(fox_s[:, :fox_w], fox_kv_s, logf_s, suf_s, cache_fox_kv, cache_fox_logf,
                             page_table, n_tok)
    o_diff = _diff_sample_call(diff_s[:, :diff_w], diff_kv_s, cache_diff_kv, page_table, rel_bias,
                               lam_diff[0], n_tok, lam_init0)
    xs = _outproj_call([o_fox, o_diff], w_out_even[0], xs, g1, n_tok, ms)
    hs_ = _normmod_call(xs, g_norm[0, 1], sc2, sh2, F32, n_tok, ms)
    xs = _swiglu_call(hs_, w_gate[0], w_up[0], w_down[0], xs, g2, n_tok, ms)

    w_in1 = w_in_odd[0]
    sh1, sc1, g1, sh2p, sc2p, g2p = mod6(1, 0, bp)
    hp = _normmod_call(xp, g_norm[1, 0], sc1, sh1, BF16, seq, TM_PROMPT)
    sb_qkv, sb_kv_p = _inproj_call(hp, w_in1, 0, 3 * sb_w, None, HEAD_DIM, True, TM_PROMPT)
    o_sb = _sb_prompt_call(sb_qkv, bp, seq, h_sb, TQ_PROMPT, 4)
    xp = _outproj_call([o_sb], w_out_odd[0], xp, g1, seq, TM_PROMPT)
    hp2 = _normmod_call(xp, g_norm[1, 1], sc2p, sh2p, F32, seq, TM_PROMPT)

    sh1, sc1, g1, sh2s, sc2s, g2s = mod6(1, bp, bp + bs)
    hs_ = _normmod_call(xs, g_norm[1, 0], sc1, sh1, F32, n_tok, ms)
    sb_s = _inproj_call(hs_, w_in1, 0, 3 * sb_w, None, HEAD_DIM, False, ms)
    sb_kv_s = sb_s[:, sb_w:]
    o_sb = _sb_sample_call(sb_s[:, :sb_w], sb_kv_s, cache_sb_kv, page_table, n_tok)
    xs = _outproj_call([o_sb], w_out_odd[0], xs, g1, n_tok, ms)
    hs2 = _normmod_call(xs, g_norm[1, 1], sc2s, sh2s, F32, n_tok, ms)

    yp, ys = _moe([hp2, hs2], [xp, xs], [g2p, g2s], [seq, n_tok], w_router[0], b_router[0],
                  w_gate_exp[0], w_up_exp[0], w_down_exp[0], TM_TOKEN, TM_GROUP)

    return (
        yp.reshape(bp, seq, d),
        ys.reshape(bs, n_tok, d),
        fox_kv_p.reshape(1, bp, seq, 2, h_fox, HEAD_DIM),
        logf_p.reshape(1, bp, seq, h_fox),
        diff_kv_p.reshape(1, bp, seq, 2, h_diff, HEAD_DIM),
        sb_kv_p.reshape(1, bp, seq, 2, h_sb, HEAD_DIM),
        fox_kv_s.reshape(1, bs, n_tok, 2, h_fox, HEAD_DIM),
        logf_s.reshape(1, bs, n_tok, h_fox),
        diff_kv_s.reshape(1, bs, n_tok, 2, h_diff, HEAD_DIM),
        sb_kv_s.reshape(1, bs, n_tok, 2, h_sb, HEAD_DIM),
    )
```

```python
import functools
import math

import numpy as np
import jax
import jax.numpy as jnp
from jax import lax
from jax.experimental import pallas as pl
from jax.experimental.pallas import tpu as pltpu

F32 = jnp.float32
BF16 = jnp.bfloat16
I32 = jnp.int32

EPS = 1e-6
NEG = -1e30
LANE = 128
HEAD_DIM = 128
DIFF_DK = 64
N_BUCKETS = 32
MAX_DISTANCE = 128
N_EXPERTS = 8
VMEM_LIMIT = 56 * 1024 * 1024

NN = ((1,), (0,))
NT = ((1,), (1,))


def _cparams(sem):
    return pltpu.CompilerParams(dimension_semantics=sem, vmem_limit_bytes=VMEM_LIMIT)


def _dot(a, b, dims=NN):
    return lax.dot_general(a, b, (dims, ((), ())), preferred_element_type=F32)


def _split2(a):
    hi = a.astype(BF16)
    lo = (a - hi.astype(F32)).astype(BF16)
    return hi, lo


def _split3(a):
    hi = a.astype(BF16)
    r = a - hi.astype(F32)
    mid = r.astype(BF16)
    lo = (r - mid.astype(F32)).astype(BF16)
    return hi, mid, lo


def _dot_sel(sel_bf16, x, dims=NN, sel_first=True):
    acc = None
    for part in _split3(x.astype(F32)):
        t = _dot(sel_bf16, part, dims) if sel_first else _dot(part, sel_bf16, dims)
        acc = t if acc is None else acc + t
    return acc


def _dot_sel2(x, sel_bf16):
    hi, lo = _split2(x)
    return _dot(hi, sel_bf16) + _dot(lo, sel_bf16)


def _mm(a, w):
    return _dot(a.astype(BF16), w.astype(BF16))


def _log_sigmoid(x):
    return jnp.minimum(x, 0.0) - jnp.log1p(jnp.exp(-jnp.abs(x)))


def _log_sigmoid_exponent(x):
    return jnp.minimum(x, 0.0) - jnp.log(1.0 + jnp.exp(-jnp.abs(x)))


def _silu(x):
    return x / (1.0 + jnp.exp(-x))


def _ada_kernel(c_ref, w_ref, b_ref, o_ref):
    o_ref[...] = _mm(_silu(c_ref[...]), w_ref[...]) + b_ref[...]


def _ada_call(c_all, w_ada, b_ada, tn=1024):
    n_layers, d, n = w_ada.shape
    mc = c_all.shape[0]
    return pl.pallas_call(
        _ada_kernel,
        grid=(n_layers, n // tn),
        in_specs=[
            pl.BlockSpec((mc, d), lambda l, j: (0, 0)),
            pl.BlockSpec((None, d, tn), lambda l, j: (l, 0, j)),
            pl.BlockSpec((None, 1, tn), lambda l, j: (l, 0, j)),
        ],
        out_specs=pl.BlockSpec((None, mc, tn), lambda l, j: (l, 0, j)),
        out_shape=jax.ShapeDtypeStruct((n_layers, mc, n), F32),
        compiler_params=_cparams(("arbitrary", "arbitrary")),
        name="ada",
    )(c_all, w_ada, b_ada.reshape(n_layers, 1, n))


def _normmod_kernel(x_ref, g_ref, sc_ref, sh_ref, o_ref):
    x = x_ref[...]
    y = x * lax.rsqrt(jnp.mean(x * x, axis=-1, keepdims=True) + EPS)
    y = y * g_ref[...]
    o_ref[...] = (y * (1.0 + sc_ref[...]) + sh_ref[...]).astype(o_ref.dtype)


def _normmod_call(x, g, sc, sh, out_dtype, rows_per_batch, tm):
    m, d = x.shape
    if rows_per_batch % tm == 0:
        per = rows_per_batch // tm
        sc_in, sh_in = sc.reshape(-1, 1, d), sh.reshape(-1, 1, d)
        mod_spec = pl.BlockSpec((None, 1, d), lambda i: (i // per, 0, 0))
    else:
        sc_in = jnp.repeat(sc, rows_per_batch, axis=0)
        sh_in = jnp.repeat(sh, rows_per_batch, axis=0)
        mod_spec = pl.BlockSpec((tm, d), lambda i: (i, 0))
    return pl.pallas_call(
        _normmod_kernel,
        grid=(m // tm,),
        in_specs=[
            pl.BlockSpec((tm, d), lambda i: (i, 0)),
            pl.BlockSpec((1, d), lambda i: (0, 0)),
            mod_spec,
            mod_spec,
        ],
        out_specs=pl.BlockSpec((tm, d), lambda i: (i, 0)),
        out_shape=jax.ShapeDtypeStruct((m, d), out_dtype),
        compiler_params=_cparams(("arbitrary",)),
        name="normmod",
    )(x, g.reshape(1, d), sc_in, sh_in)


def _group_rms(y, gain, group):
    outs = []
    lane = lax.broadcasted_iota(I32, (1, LANE), 1)
    for c in range(y.shape[1] // LANE):
        blk = y[:, c * LANE:(c + 1) * LANE]
        sq = blk * blk
        if group == LANE:
            ms = jnp.mean(sq, axis=-1, keepdims=True)
        else:
            lo_half = lane < group
            s0 = jnp.sum(jnp.where(lo_half, sq, 0.0), axis=-1, keepdims=True)
            s1 = jnp.sum(jnp.where(lo_half, 0.0, sq), axis=-1, keepdims=True)
            ms = jnp.where(lo_half, s0, s1) * (1.0 / group)
        outs.append(blk * lax.rsqrt(ms + EPS) * gain[:, c * LANE:(c + 1) * LANE])
    return jnp.concatenate(outs, axis=-1) if len(outs) > 1 else outs[0]


def _inproj_kernel(a_ref, w_ref, gain_ref, *out_refs, group, n_norm_tiles, n_q_tiles,
                   with_bf16):
    j = pl.program_id(1)
    y = _mm(a_ref[...], w_ref[...])
    if with_bf16:
        qkv_ref, kv_ref = out_refs
    else:
        (qkv_ref,), kv_ref = out_refs, None

    def store(val):
        qkv_ref[...] = val.astype(qkv_ref.dtype)
        if kv_ref is not None:
            @pl.when(j >= n_q_tiles)
            def _():
                kv_ref[...] = val

    if n_norm_tiles == 0:
        store(y)
    else:
        @pl.when(j < n_norm_tiles)
        def _():
            store(_group_rms(y, gain_ref[...], group))

        @pl.when(j >= n_norm_tiles)
        def _():
            store(y)


def _inproj_call(h, w, col_off, n_cols, gain, group, with_bf16, tm, tn=512):
    m, k = h.shape
    assert col_off % tn == 0 and n_cols % (3 * tn) == 0
    n_tiles = n_cols // tn
    n_q_tiles = n_tiles // 3
    n_norm_tiles = 0 if gain is None else 2 * n_q_tiles
    if gain is None:
        gain = jnp.ones((1, n_cols), F32)
    off = col_off // tn
    kern = functools.partial(_inproj_kernel, group=group, n_norm_tiles=n_norm_tiles,
                             n_q_tiles=n_q_tiles, with_bf16=with_bf16)
    if with_bf16:
        out_specs = [pl.BlockSpec((tm, tn), lambda i, j: (i, j)),
                     pl.BlockSpec((tm, tn), lambda i, j: (i, jnp.maximum(j - n_q_tiles, 0)))]
        out_shape = [jax.ShapeDtypeStruct((m, n_cols), BF16),
                     jax.ShapeDtypeStruct((m, n_cols - n_cols // 3), F32)]
    else:
        out_specs = [pl.BlockSpec((tm, tn), lambda i, j: (i, j))]
        out_shape = [jax.ShapeDtypeStruct((m, n_cols), F32)]
    res = pl.pallas_call(
        kern,
        grid=(m // tm, n_tiles),
        in_specs=[
            pl.BlockSpec((tm, k), lambda i, j: (i, 0)),
            pl.BlockSpec((k, tn), lambda i, j: (0, j + off)),
            pl.BlockSpec((1, tn), lambda i, j: (0, j)),
        ],
        out_specs=out_specs,
        out_shape=out_shape,
        compiler_params=_cparams(("arbitrary", "arbitrary")),
        name="inproj",
    )(h, w, gain)
    return res if with_bf16 else res[0]


def _logf_kernel(a_ref, w_ref, b_ref, lf_ref, ck_ref, suf_ref, carry_ref, *, seg):
    i = pl.program_id(0)
    tm = a_ref.shape[0]
    lf = _log_sigmoid(_mm(a_ref[...], w_ref[...]) + b_ref[...])
    lf_ref[...] = lf
    shift = int(math.log2(min(seg, tm)))
    r = lax.broadcasted_iota(I32, (tm, tm), 0)
    c = lax.broadcasted_iota(I32, (tm, tm), 1)
    same = lax.shift_right_logical(r, shift) == lax.shift_right_logical(c, shift)
    lower = jnp.where(same & (c <= r), 1.0, 0.0).astype(BF16)
    upper = jnp.where(same & (c > r), 1.0, 0.0).astype(BF16)
    pre = _dot_sel(lower, lf)
    suf_ref[...] = _dot_sel(upper, lf)
    if seg > tm:
        per = seg // tm

        @pl.when(i % per == 0)
        def _():
            carry_ref[...] = jnp.zeros_like(carry_ref)

        pre = pre + carry_ref[...]
        carry_ref[...] = pre[tm - 1:tm, :]
    ck_ref[...] = pre


def _logf_call(h, w_fg, b_f, seg, tm):
    m, k = h.shape
    nh = w_fg.shape[1]
    w_pad = jnp.pad(w_fg, ((0, 0), (0, LANE - nh)))
    b_pad = jnp.pad(b_f.reshape(1, nh), ((0, 0), (0, LANE - nh)))
    spec = pl.BlockSpec((tm, LANE), lambda i: (i, 0))
    shp = jax.ShapeDtypeStruct((m, LANE), F32)
    lf, ck, suf = pl.pallas_call(
        functools.partial(_logf_kernel, seg=seg),
        grid=(m // tm,),
        in_specs=[
            pl.BlockSpec((tm, k), lambda i: (i, 0)),
            pl.BlockSpec((k, LANE), lambda i: (0, 0)),
            pl.BlockSpec((1, LANE), lambda i: (0, 0)),
        ],
        out_specs=[spec, spec, spec],
        out_shape=[shp, shp, shp],
        scratch_shapes=[pltpu.VMEM((1, LANE), F32)],
        compiler_params=_cparams(("arbitrary",)),
        name="logf",
    )(h, w_pad, b_pad)
    return lf[:, :nh], ck[:, :nh], suf[:, :nh]


def _lane_tile(x, width):
    reps = width // LANE
    return jnp.concatenate([x] * reps, axis=1) if reps > 1 else x


def _tri_pairs(n, descending):
    qi, kj = [], []
    for a in range(n):
        ks = range(a, -1, -1) if descending else range(a + 1)
        for b in ks:
            qi.append(a)
            kj.append(b)
    return jnp.asarray(qi, I32), jnp.asarray(kj, I32)


def _fox_prompt_kernel(qi_tab, kj_tab, q_ref, k_ref, v_ref, cq_ref, ck_ref, o_ref,
                       m_scr, l_scr, acc_scr, *, n_heads, scale):
    p = pl.program_id(1)
    qi, kj = qi_tab[p], kj_tab[p]
    tq, tk = q_ref.shape[0], k_ref.shape[0]

    @pl.when(kj == 0)
    def _():
        m_scr[...] = jnp.full_like(m_scr, NEG)
        l_scr[...] = jnp.zeros_like(l_scr)
        acc_scr[...] = jnp.zeros_like(acc_scr)

    def step(masked):
        if masked:
            causal = (lax.broadcasted_iota(I32, (tq, tk), 1)
                      <= lax.broadcasted_iota(I32, (tq, tk), 0))
        for h in range(n_heads):
            hs = slice(h * HEAD_DIM, (h + 1) * HEAD_DIM)
            s = _dot(q_ref[:, hs], k_ref[:, hs], NT) * scale
            s = s + (_lane_tile(cq_ref[:, hs], tk) - ck_ref[h:h + 1, :])
            if masked:
                s = jnp.where(causal, s, NEG)
            m_prev = m_scr[h]
            m_new = jnp.maximum(m_prev, jnp.max(s, axis=-1, keepdims=True))
            alpha = jnp.exp(m_prev - m_new)
            pr = jnp.exp(s - _lane_tile(m_new, tk))
            l_scr[h] = alpha * l_scr[h] + jnp.sum(pr, axis=-1, keepdims=True)
            acc_scr[:, hs] = alpha * acc_scr[:, hs] + _dot(pr.astype(BF16), v_ref[:, hs])
            m_scr[h] = m_new

    @pl.when(kj < qi)
    def _():
        step(False)

    @pl.when(kj == qi)
    def _():
        step(True)
        for h in range(n_heads):
            hs = slice(h * HEAD_DIM, (h + 1) * HEAD_DIM)
            o_ref[:, hs] = (acc_scr[:, hs] / l_scr[h]).astype(o_ref.dtype)


def _fox_prompt_call(qkv, cq, ck_rows, batch, seq, n_heads, tq):
    w = n_heads * HEAD_DIM
    nq = seq // tq
    qi_tab, kj_tab = _tri_pairs(nq, descending=False)
    grid_spec = pltpu.PrefetchScalarGridSpec(
        num_scalar_prefetch=2,
        grid=(batch, int(qi_tab.shape[0])),
        in_specs=[
            pl.BlockSpec((tq, w), lambda b, p, qt, kt: (b * nq + qt[p], 0)),
            pl.BlockSpec((tq, w), lambda b, p, qt, kt: (b * nq + kt[p], 1)),
            pl.BlockSpec((tq, w), lambda b, p, qt, kt: (b * nq + kt[p], 2)),
            pl.BlockSpec((tq, w), lambda b, p, qt, kt: (b * nq + qt[p], 0)),
            pl.BlockSpec((None, n_heads, tq), lambda b, p, qt, kt: (b, 0, kt[p])),
        ],
        out_specs=pl.BlockSpec((tq, w), lambda b, p, qt, kt: (b * nq + qt[p], 0)),
        scratch_shapes=[pltpu.VMEM((n_heads, tq, LANE), F32), pltpu.VMEM((n_heads, tq, LANE), F32),
                        pltpu.VMEM((tq, w), F32)],
    )
    return pl.pallas_call(
        functools.partial(_fox_prompt_kernel, n_heads=n_heads, scale=HEAD_DIM ** -0.5),
        grid_spec=grid_spec,
        out_shape=jax.ShapeDtypeStruct((batch * seq, w), BF16),
        compiler_params=_cparams(("arbitrary", "arbitrary")),
        name="fox_prompt",
    )(qi_tab, kj_tab, qkv, qkv, qkv, cq, ck_rows)


def _t5_bucket(n):
    n = jnp.maximum(n, 0)
    exact = N_BUCKETS // 2
    nf = jnp.maximum(n, 1).astype(F32)
    large = exact + (jnp.log(nf / exact) / math.log(MAX_DISTANCE / exact)
                     * (N_BUCKETS - exact)).astype(I32)
    return jnp.where(n < exact, n, jnp.minimum(large, N_BUCKETS - 1))


def _bias_tile_kernel(rb_ref, o_ref, *, t):
    i = lax.broadcasted_iota(I32, (t, t), 0)
    j = lax.broadcasted_iota(I32, (t, t), 1)
    d = i - j
    bucket = _t5_bucket(jnp.where(d < 0, d + t, d))
    h = pl.program_id(0)
    acc = jnp.zeros((t, t), F32)
    for b in range(N_BUCKETS):
        acc = jnp.where(bucket == b, rb_ref[b, h], acc)
    o_ref[...] = acc


def _bias_tile_call(rel_bias, t):
    n_heads = rel_bias.shape[1]
    return pl.pallas_call(
        functools.partial(_bias_tile_kernel, t=t),
        grid=(n_heads,),
        in_specs=[pl.BlockSpec(memory_space=pltpu.SMEM)],
        out_specs=pl.BlockSpec((None, t, t), lambda h: (h, 0, 0)),
        out_shape=jax.ShapeDtypeStruct((n_heads, t, t), F32),
        compiler_params=_cparams(("arbitrary",)),
        name="bias_tile",
    )(rel_bias)


def _diff_lambda(lam_ref, lam_init):
    lp = lam_ref[...]
    a = jnp.sum(lp[0:1, :] * lp[1:2, :], axis=-1, keepdims=True)
    b = jnp.sum(lp[2:3, :] * lp[3:4, :], axis=-1, keepdims=True)
    return jnp.exp(a) - jnp.exp(b) + lam_init


def _diff_prompt_kernel(qi_tab, kj_tab, q_ref, k_ref, v_ref, tt_ref, lam_ref, o_ref,
                        m_scr, l_scr, acc_scr, *, n_heads, scale, lam_init):
    p = pl.program_id(1)
    qi, kj = qi_tab[p], kj_tab[p]
    tq, tk = q_ref.shape[0], k_ref.shape[0]

    @pl.when(kj == 0)
    def _():
        m_scr[...] = jnp.full_like(m_scr, NEG)
        l_scr[...] = jnp.zeros_like(l_scr)
        acc_scr[...] = jnp.zeros_like(acc_scr)

    def step(diag):
        row = lax.broadcasted_iota(I32, (tq, tk), 0)
        col = lax.broadcasted_iota(I32, (tq, tk), 1)
        lane = lax.broadcasted_iota(I32, (1, HEAD_DIM), 1)
        first_half = lane < DIFF_DK
        if diag:
            causal = col <= row
        else:
            near = col > row + jnp.where(kj == qi - 1, 0, tk)
        for h in range(n_heads):
            hs = slice(h * HEAD_DIM, (h + 1) * HEAD_DIM)
            q = q_ref[:, hs]
            zero = jnp.zeros_like(q)
            q2 = jnp.concatenate([jnp.where(first_half, q, zero),
                                  jnp.where(first_half, zero, q)], axis=0)
            s = _dot(q2, k_ref[:, hs], NT) * scale
            tt = tt_ref[h]
            if diag:
                bias = tt
            else:
                bias = jnp.where(near, tt, tt_ref[h, tq - 1:tq, 0:1])
            for c in range(2):
                rs = slice(c * tq, (c + 1) * tq)
                idx = 2 * h + c
                sc = s[rs] + bias
                if diag:
                    sc = jnp.where(causal, sc, NEG)
                m_prev = m_scr[idx]
                m_new = jnp.maximum(m_prev, jnp.max(sc, axis=-1, keepdims=True))
                alpha = jnp.exp(m_prev - m_new)
                pr = jnp.exp(sc - _lane_tile(m_new, tk))
                l_scr[idx] = alpha * l_scr[idx] + jnp.sum(pr, axis=-1, keepdims=True)
                acc_scr[c, :, hs] = alpha * acc_scr[c, :, hs] + _dot(pr.astype(BF16), v_ref[:, hs])
                m_scr[idx] = m_new

    @pl.when(kj < qi)
    def _():
        step(False)

    @pl.when(kj == qi)
    def _():
        step(True)
        lam = _diff_lambda(lam_ref, lam_init)
        for h in range(n_heads):
            hs = slice(h * HEAD_DIM, (h + 1) * HEAD_DIM)
            o0 = acc_scr[0, :, hs] / l_scr[2 * h]
            o1 = acc_scr[1, :, hs] / l_scr[2 * h + 1]
            o = o0 - lam * o1
            o = o * lax.rsqrt(jnp.mean(o * o, axis=-1, keepdims=True) + EPS) * (1.0 - lam_init)
            o_ref[:, hs] = o.astype(o_ref.dtype)


def _diff_prompt_call(qkv, tt, lam_diff, batch, seq, n_heads, tq, lam_init):
    w = n_heads * HEAD_DIM
    nq = seq // tq
    qi_tab, kj_tab = _tri_pairs(nq, descending=False)
    grid_spec = pltpu.PrefetchScalarGridSpec(
        num_scalar_prefetch=2,
        grid=(batch, int(qi_tab.shape[0])),
        in_specs=[
            pl.BlockSpec((tq, w), lambda b, p, qt, kt: (b * nq + qt[p], 0)),
            pl.BlockSpec((tq, w), lambda b, p, qt, kt: (b * nq + kt[p], 1)),
            pl.BlockSpec((tq, w), lambda b, p, qt, kt: (b * nq + kt[p], 2)),
            pl.BlockSpec((n_heads, tq, tq), lambda b, p, qt, kt: (0, 0, 0)),
            pl.BlockSpec(lam_diff.shape, lambda b, p, qt, kt: (0, 0)),
        ],
        out_specs=pl.BlockSpec((tq, w), lambda b, p, qt, kt: (b * nq + qt[p], 0)),
        scratch_shapes=[pltpu.VMEM((2 * n_heads, tq, LANE), F32),
                        pltpu.VMEM((2 * n_heads, tq, LANE), F32),
                        pltpu.VMEM((2, tq, w), F32)],
    )
    return pl.pallas_call(
        functools.partial(_diff_prompt_kernel, n_heads=n_heads, scale=DIFF_DK ** -0.5,
                          lam_init=lam_init),
        grid_spec=grid_spec,
        out_shape=jax.ShapeDtypeStruct((batch * seq, w), BF16),
        compiler_params=_cparams(("arbitrary", "arbitrary")),
        name="diff_prompt",
    )(qi_tab, kj_tab, qkv, qkv, qkv, tt, lam_diff)


def _suffix_matrix():
    l = lax.broadcasted_iota(I32, (LANE, 2 * LANE), 0)
    j = lax.broadcasted_iota(I32, (LANE, 2 * LANE), 1)
    return jnp.where((l > j) | (j >= LANE), 1.0, 0.0).astype(BF16)


def _sb_tile(z, valid, v, r, u2):
    tk = z.shape[1]
    chunks = [None] * (tk // LANE)
    for c in reversed(range(tk // LANE)):
        zc = z[:, c * LANE:(c + 1) * LANE]
        ls = _log_sigmoid_exponent(zc)
        lr = ls - zc
        if valid is not None:
            vc = valid[:, c * LANE:(c + 1) * LANE]
            lr = jnp.where(vc, lr, 0.0)
        hi, lo = _split2(lr)
        s2 = _dot(hi, u2) + _dot(lo, u2)
        a = jnp.exp(ls + (s2[:, :LANE] + r))
        if valid is not None:
            a = jnp.where(vc, a, 0.0)
        chunks[c] = a.astype(BF16)
        r = r + s2[:, LANE:]
    a_full = jnp.concatenate(chunks, axis=-1) if len(chunks) > 1 else chunks[0]
    return _dot(a_full, v), r


def _sb_prompt_kernel(qi_tab, kj_tab, q_ref, k_ref, v_ref, o_ref, r_scr, acc_scr, *,
                      n_heads, scale):
    p = pl.program_id(2)
    qi, kj = qi_tab[p], kj_tab[p]
    tq, tk = q_ref.shape[0], k_ref.shape[0]
    u2 = _suffix_matrix()

    def step(diag):
        valid = None
        if diag:
            valid = (lax.broadcasted_iota(I32, (tq, tk), 1)
                     < lax.broadcasted_iota(I32, (tq, tk), 0))
        for h in range(n_heads):
            hs = slice(h * HEAD_DIM, (h + 1) * HEAD_DIM)
            z = _dot(q_ref[:, hs], k_ref[:, hs], NT) * scale
            if diag:
                r0 = jnp.zeros((tq, LANE), F32)
                pv, r1 = _sb_tile(z, valid, v_ref[:, hs], r0, u2)
                acc_scr[:, hs] = pv
            else:
                pv, r1 = _sb_tile(z, None, v_ref[:, hs], r_scr[h], u2)
                acc_scr[:, hs] = acc_scr[:, hs] + pv
            r_scr[h] = r1

    @pl.when(kj == qi)
    def _():
        step(True)

    @pl.when(kj < qi)
    def _():
        step(False)

    @pl.when(kj == 0)
    def _():
        o_ref[...] = acc_scr[...].astype(o_ref.dtype)


def _sb_prompt_call(qkv, batch, seq, n_heads, tq, heads_per_step):
    n_groups = n_heads // heads_per_step
    w = heads_per_step * HEAD_DIM
    nq = seq // tq
    qi_tab, kj_tab = _tri_pairs(nq, descending=True)
    grid_spec = pltpu.PrefetchScalarGridSpec(
        num_scalar_prefetch=2,
        grid=(batch, n_groups, int(qi_tab.shape[0])),
        in_specs=[
            pl.BlockSpec((tq, w), lambda b, g, p, qt, kt: (b * nq + qt[p], g)),
            pl.BlockSpec((tq, w), lambda b, g, p, qt, kt: (b * nq + kt[p], n_groups + g)),
            pl.BlockSpec((tq, w), lambda b, g, p, qt, kt: (b * nq + kt[p], 2 * n_groups + g)),
        ],
        out_specs=pl.BlockSpec((tq, w), lambda b, g, p, qt, kt: (b * nq + qt[p], g)),
        scratch_shapes=[pltpu.VMEM((heads_per_step, tq, LANE), F32), pltpu.VMEM((tq, w), F32)],
    )
    return pl.pallas_call(
        functools.partial(_sb_prompt_kernel, n_heads=heads_per_step, scale=HEAD_DIM ** -0.5),
        grid_spec=grid_spec,
        out_shape=jax.ShapeDtypeStruct((batch * seq, n_heads * HEAD_DIM), BF16),
        compiler_params=_cparams(("arbitrary", "arbitrary", "arbitrary")),
        name="sb_prompt",
    )(qi_tab, kj_tab, qkv, qkv, qkv)


def _outproj_kernel(*refs, n_parts):
    a_refs = refs[:n_parts]
    w_refs = refs[n_parts:2 * n_parts]
    x_ref, g_ref, o_ref = refs[2 * n_parts:]
    y = None
    for a_ref, w_ref in zip(a_refs, w_refs):
        t = _mm(a_ref[...], w_ref[...])
        y = t if y is None else y + t
    o_ref[...] = x_ref[...] + g_ref[...] * y


def _gate_spec(g, rows_per_batch, tm, tn, col_tiled):
    d = g.shape[1]
    col = (lambda j: j) if col_tiled else (lambda j: 0)
    if rows_per_batch % tm == 0:
        per = rows_per_batch // tm
        return g.reshape(-1, 1, d), pl.BlockSpec((None, 1, tn), lambda i, j: (i // per, 0, col(j)))
    return (jnp.repeat(g, rows_per_batch, axis=0),
            pl.BlockSpec((tm, tn), lambda i, j: (i, col(j))))


def _outproj_call(parts, w, x, g, rows_per_batch, tm, tn=512):
    m, d = x.shape
    kp = parts[0].shape[1]
    n_parts = len(parts)
    g_in, g_spec = _gate_spec(g, rows_per_batch, tm, tn, True)
    in_specs = [pl.BlockSpec((tm, kp), lambda i, j: (i, 0)) for _ in parts]
    in_specs += [pl.BlockSpec((kp, tn), functools.partial(lambda i, j, c: (c, j), c=c))
                 for c in range(n_parts)]
    in_specs += [pl.BlockSpec((tm, tn), lambda i, j: (i, j)), g_spec]
    return pl.pallas_call(
        functools.partial(_outproj_kernel, n_parts=n_parts),
        grid=(m // tm, d // tn),
        in_specs=in_specs,
        out_specs=pl.BlockSpec((tm, tn), lambda i, j: (i, j)),
        out_shape=jax.ShapeDtypeStruct((m, d), F32),
        compiler_params=_cparams(("arbitrary", "arbitrary")),
        name="outproj",
    )(*parts, *([w] * n_parts), x, g_in)


def _swiglu_kernel(h_ref, wg_ref, wu_ref, wd_ref, x_ref, g_ref, o_ref):
    j = pl.program_id(1)
    h = h_ref[...]
    gate = _mm(h, wg_ref[...])
    up = _mm(h, wu_ref[...])
    part = _mm(_silu(gate) * up, wd_ref[...])

    @pl.when(j == 0)
    def _():
        o_ref[...] = part

    @pl.when(j > 0)
    def _():
        o_ref[...] = o_ref[...] + part

    @pl.when(j == pl.num_programs(1) - 1)
    def _():
        o_ref[...] = x_ref[...] + g_ref[...] * o_ref[...]


def _swiglu_call(h, wg, wu, wd, x, g, rows_per_batch, tm, tf=256):
    m, d = x.shape
    f = wg.shape[1]
    g_in, g_spec = _gate_spec(g, rows_per_batch, tm, d, False)
    return pl.pallas_call(
        _swiglu_kernel,
        grid=(m // tm, f // tf),
        in_specs=[
            pl.BlockSpec((tm, d), lambda i, j: (i, 0)),
            pl.BlockSpec((d, tf), lambda i, j: (0, j)),
            pl.BlockSpec((d, tf), lambda i, j: (0, j)),
            pl.BlockSpec((tf, d), lambda i, j: (j, 0)),
            pl.BlockSpec((tm, d), lambda i, j: (i, 0), pipeline_mode=pl.Buffered(1)),
            g_spec,
        ],
        out_specs=pl.BlockSpec((tm, d), lambda i, j: (i, 0)),
        out_shape=jax.ShapeDtypeStruct((m, d), F32),
        compiler_params=_cparams(("arbitrary", "arbitrary")),
        name="swiglu",
    )(h, wg, wu, wd, x, g_in)


def _router_kernel(h_ref, w_ref, b_ref, id_ref, wt_ref):
    logits = _mm(h_ref[...], w_ref[...]) + b_ref[...]
    lane = lax.broadcasted_iota(I32, logits.shape, 1)
    logits = jnp.where(lane < N_EXPERTS, logits, -jnp.inf)
    m1 = jnp.max(logits, axis=-1, keepdims=True)
    i1 = jnp.min(jnp.where(logits == m1, lane, LANE), axis=-1, keepdims=True)
    rest = jnp.where(lane == i1, -jnp.inf, logits)
    m2 = jnp.max(rest, axis=-1, keepdims=True)
    i2 = jnp.min(jnp.where(rest == m2, lane, LANE), axis=-1, keepdims=True)
    e = jnp.exp(m2 - m1)
    w1 = 1.0 / (1.0 + e)
    w2 = e / (1.0 + e)
    id_ref[...] = jnp.where(lane == 0, i1, jnp.where(lane == 1, i2, 0))
    wt_ref[...] = jnp.where(lane == 0, w1, jnp.where(lane == 1, w2, 0.0))


def _router_call(h, w_router, b_router, tm):
    m, d = h.shape
    ne = w_router.shape[1]
    w_pad = jnp.pad(w_router, ((0, 0), (0, LANE - ne)))
    b_pad = jnp.pad(b_router.reshape(1, ne), ((0, 0), (0, LANE - ne)))
    spec = pl.BlockSpec((tm, LANE), lambda i: (i, 0))
    ids, wts = pl.pallas_call(
        _router_kernel,
        grid=(m // tm,),
        in_specs=[
            pl.BlockSpec((tm, d), lambda i: (i, 0)),
            pl.BlockSpec((d, LANE), lambda i: (0, 0)),
            pl.BlockSpec((1, LANE), lambda i: (0, 0)),
        ],
        out_specs=[spec, spec],
        out_shape=[jax.ShapeDtypeStruct((m, LANE), I32), jax.ShapeDtypeStruct((m, LANE), F32)],
        compiler_params=_cparams(("arbitrary",)),
        name="router",
    )(h, w_pad, b_pad)
    return ids[:, :2], wts[:, :2]


def _rank_kernel(id_ref, rank_ref, cnt_ref, carry_ref):
    i = pl.program_id(0)
    tm = id_ref.shape[0]

    @pl.when(i == 0)
    def _():
        carry_ref[...] = jnp.zeros_like(carry_ref)

    ids = id_ref[...]
    lane = lax.broadcasted_iota(I32, ids.shape, 1)
    oh0 = jnp.where(lane == ids[:, 0:1], 1.0, 0.0)
    oh1 = jnp.where(lane == ids[:, 1:2], 1.0, 0.0)
    sel = oh0 + oh1
    r = lax.broadcasted_iota(I32, (tm, tm), 0)
    c = lax.broadcasted_iota(I32, (tm, tm), 1)
    strict_lower = jnp.where(c < r, 1.0, 0.0).astype(BF16)
    pre = _dot(strict_lower, sel.astype(BF16)) + carry_ref[...]
    rank0 = jnp.sum(oh0 * pre, axis=-1, keepdims=True)
    rank1 = jnp.sum(oh1 * pre, axis=-1, keepdims=True)
    rank_ref[...] = jnp.where(lane == 0, rank0, jnp.where(lane == 1, rank1, 0.0)).astype(I32)
    total = pre[tm - 1:tm, :] + sel[tm - 1:tm, :]
    carry_ref[...] = total
    cnt_ref[...] = total


def _rank_call(ids, tm=128):
    t = ids.shape[0]
    ids_pad = jnp.pad(ids, ((0, 0), (0, LANE - 2)), constant_values=-1)
    rank, cnt = pl.pallas_call(
        _rank_kernel,
        grid=(t // tm,),
        in_specs=[pl.BlockSpec((tm, LANE), lambda i: (i, 0))],
        out_specs=[pl.BlockSpec((tm, LANE), lambda i: (i, 0)),
                   pl.BlockSpec((1, LANE), lambda i: (0, 0))],
        out_shape=[jax.ShapeDtypeStruct((t, LANE), I32), jax.ShapeDtypeStruct((1, LANE), F32)],
        scratch_shapes=[pltpu.VMEM((1, LANE), F32)],
        compiler_params=_cparams(("arbitrary",)),
        name="rank",
    )(ids_pad)
    return rank[:, :2], cnt[0, :N_EXPERTS].astype(I32)


def _row_copy(src_ref, src_row, dst_ref, dst_row, sem):
    return pltpu.make_async_copy(src_ref.at[pl.ds(src_row, 1)], dst_ref.at[pl.ds(dst_row, 1)], sem)


def _dispatch_kernel(dest_ref, h_ref, hs_in_ref, hs_ref, sem):
    del hs_in_ref
    n = h_ref.shape[0]

    def start(r, carry):
        _row_copy(h_ref, r, hs_ref, dest_ref[0, 0, 2 * r], sem).start()
        _row_copy(h_ref, r, hs_ref, dest_ref[0, 0, 2 * r + 1], sem).start()
        return carry

    lax.fori_loop(0, n, start, 0)

    def wait(r, carry):
        _row_copy(h_ref, 0, hs_ref, 0, sem).wait()
        _row_copy(h_ref, 0, hs_ref, 0, sem).wait()
        return carry

    lax.fori_loop(0, n, wait, 0)


def _dispatch_call(h, dest, hs, tm):
    t, d = h.shape
    dest3 = dest.reshape(t // tm, 1, 2 * tm)
    return pl.pallas_call(
        _dispatch_kernel,
        grid=(t // tm,),
        in_specs=[
            pl.BlockSpec((1, 1, 2 * tm), lambda i: (i, 0, 0), memory_space=pltpu.SMEM),
            pl.BlockSpec((tm, d), lambda i: (i, 0)),
            pl.BlockSpec(memory_space=pl.ANY),
        ],
        out_specs=pl.BlockSpec(memory_space=pl.ANY),
        out_shape=jax.ShapeDtypeStruct(hs.shape, hs.dtype),
        scratch_shapes=[pltpu.SemaphoreType.DMA],
        input_output_aliases={2: 0},
        compiler_params=_cparams(("arbitrary",)),
        name="dispatch",
    )(dest3, h, hs)


def _grouped_kernel(te_tab, nr_tab, nv_tab, hs_ref, wg_ref, wu_ref, wd_ref, ys_ref,
                    x_scr, wg_scr, wu_scr, wd_scr):
    i, j = pl.program_id(0), pl.program_id(1)
    n_rows = nr_tab[i]

    @pl.when(n_rows > 0)
    def _():
        wg_scr[...] = wg_ref[...].astype(BF16)
        wu_scr[...] = wu_ref[...].astype(BF16)
        wd_scr[...] = wd_ref[...].astype(BF16)

    for sub in range(hs_ref.shape[0] // GROUP_SUB):
        rows = pl.ds(sub * GROUP_SUB, GROUP_SUB)

        @pl.when(sub * GROUP_SUB < n_rows)
        def _():
            @pl.when(j == 0)
            def _():
                x_scr[rows, :] = hs_ref[rows, :].astype(BF16)

            x = x_scr[rows, :]
            gate = _dot(x, wg_scr[...])
            up = _dot(x, wu_scr[...])
            part = _dot((_silu(gate) * up).astype(BF16), wd_scr[...])

            @pl.when(j == 0)
            def _():
                ys_ref[rows, :] = part

            @pl.when(j > 0)
            def _():
                ys_ref[rows, :] = ys_ref[rows, :] + part

        @pl.when(jnp.logical_and(sub * GROUP_SUB >= n_rows, j == 0))
        def _():
            ys_ref[rows, :] = jnp.zeros((GROUP_SUB, ys_ref.shape[1]), F32)


def _grouped_call(hs, tile_expert, tile_rows, n_valid, wg, wu, wd, tm, tf=256):
    r, d = hs.shape
    f = wg.shape[2]
    n_tiles, nf = r // tm, f // tf

    def row_map(i, j, te, nr, nv):
        return (jnp.minimum(i, nv[0] - 1), 0)

    def col_blk(i, j, nv):
        return jnp.where(i < nv[0], j, nf - 1)

    grid_spec = pltpu.PrefetchScalarGridSpec(
        num_scalar_prefetch=3,
        grid=(n_tiles, nf),
        in_specs=[
            pl.BlockSpec((tm, d), row_map, pipeline_mode=pl.Buffered(1)),
            pl.BlockSpec((None, d, tf), lambda i, j, te, nr, nv: (te[i], 0, col_blk(i, j, nv))),
            pl.BlockSpec((None, d, tf), lambda i, j, te, nr, nv: (te[i], 0, col_blk(i, j, nv))),
            pl.BlockSpec((None, tf, d), lambda i, j, te, nr, nv: (te[i], col_blk(i, j, nv), 0)),
        ],
        out_specs=pl.BlockSpec((tm, d), lambda i, j, te, nr, nv: (i, 0)),
        scratch_shapes=[pltpu.VMEM((tm, d), BF16), pltpu.VMEM((d, tf), BF16),
                        pltpu.VMEM((d, tf), BF16), pltpu.VMEM((tf, d), BF16)],
    )
    return pl.pallas_call(
        _grouped_kernel,
        grid_spec=grid_spec,
        out_shape=jax.ShapeDtypeStruct((r, d), F32),
        compiler_params=_cparams(("arbitrary", "arbitrary")),
        name="grouped_ffn",
    )(tile_expert, tile_rows, n_valid, hs, wg, wu, wd)


def _combine_kernel(dest_ref, wt_ref, x_ref, g_ref, ys_ref, o_ref, buf, sem):
    n = x_ref.shape[0]

    def start(r, carry):
        _row_copy(ys_ref, dest_ref[0, 0, 2 * r], buf.at[0], r, sem).start()
        _row_copy(ys_ref, dest_ref[0, 0, 2 * r + 1], buf.at[1], r, sem).start()
        return carry

    lax.fori_loop(0, n, start, 0)

    def wait(r, carry):
        _row_copy(ys_ref, 0, buf.at[0], 0, sem).wait()
        _row_copy(ys_ref, 0, buf.at[1], 0, sem).wait()
        return carry

    lax.fori_loop(0, n, wait, 0)
    wt = wt_ref[...]
    f = wt[:, 0:1] * buf[0] + wt[:, 1:2] * buf[1]
    o_ref[...] = x_ref[...] + g_ref[...] * f


def _combine_call(ys, dest, wts, x, g, rows_per_batch, tm):
    t, d = x.shape
    dest3 = dest.reshape(t // tm, 1, 2 * tm)
    wt_pad = jnp.pad(wts, ((0, 0), (0, LANE - 2)))
    if rows_per_batch % tm == 0:
        per = rows_per_batch // tm
        g_in, g_spec = g.reshape(-1, 1, d), pl.BlockSpec((None, 1, d), lambda i: (i // per, 0, 0))
    else:
        g_in, g_spec = jnp.repeat(g, rows_per_batch, axis=0), pl.BlockSpec((tm, d), lambda i: (i, 0))
    return pl.pallas_call(
        _combine_kernel,
        grid=(t // tm,),
        in_specs=[
            pl.BlockSpec((1, 1, 2 * tm), lambda i: (i, 0, 0), memory_space=pltpu.SMEM),
            pl.BlockSpec((tm, LANE), lambda i: (i, 0)),
            pl.BlockSpec((tm, d), lambda i: (i, 0)),
            g_spec,
            pl.BlockSpec(memory_space=pl.ANY),
        ],
        out_specs=pl.BlockSpec((tm, d), lambda i: (i, 0)),
        out_shape=jax.ShapeDtypeStruct((t, d), F32),
        scratch_shapes=[pltpu.VMEM((2, tm, d), F32), pltpu.SemaphoreType.DMA],
        compiler_params=_cparams(("arbitrary",)),
        name="combine",
    )(dest3, wt_pad, x, g_in, ys)


def _moe(h_groups, x_groups, g_groups, rows_per_batch, w_router, b_router, wg, wu, wd,
         tm_tok, tm_grp):
    d = h_groups[0].shape[1]
    routed = [_router_call(h, w_router, b_router, tm_tok) for h in h_groups]
    ids = jnp.concatenate([r[0] for r in routed], axis=0)
    n_tok = ids.shape[0]
    rank, counts = _rank_call(ids)
    padded = (counts + tm_grp - 1) // tm_grp * tm_grp
    ends = jnp.cumsum(padded)
    starts = ends - padded
    dest = starts[ids] + rank
    n_tiles = -(-2 * n_tok // tm_grp) + N_EXPERTS
    tile_row = jnp.arange(n_tiles, dtype=I32) * tm_grp
    tile_expert = jnp.minimum(jnp.sum(tile_row[:, None] >= ends[None, :], axis=1),
                              N_EXPERTS - 1).astype(I32)
    n_valid = (ends[-1] // tm_grp).astype(I32).reshape(1)
    tile_expert = jnp.where(tile_row < ends[-1], tile_expert,
                            tile_expert[jnp.maximum(n_valid[0] - 1, 0)])
    tile_rows = jnp.clip((starts + counts)[tile_expert] - tile_row, 0, tm_grp)
    tile_rows = jnp.where(tile_row < ends[-1], tile_rows, 0).astype(I32)
    hs = jnp.zeros((n_tiles * tm_grp, d), F32)
    off = 0
    for h in h_groups:
        n = h.shape[0]
        hs = _dispatch_call(h, dest[off:off + n], hs, tm_tok)
        off += n
    ys = _grouped_call(hs, tile_expert, tile_rows, n_valid, wg, wu, wd, tm_grp)
    outs, off = [], 0
    for (ids_g, wts_g), x, g, rpb in zip(routed, x_groups, g_groups, rows_per_batch):
        n = x.shape[0]
        outs.append(_combine_call(ys, dest[off:off + n], wts_g, x, g, rpb, tm_tok))
        off += n
    return outs


PAGES_PER_STEP = 8


def _rows2d(ref3):
    k, h, d = ref3.shape
    return ref3[...].reshape(k * h, d)


def _new_rows(nkv_ref, which, n_rows):
    x = nkv_ref[:, which].reshape(n_rows, HEAD_DIM)
    return jnp.concatenate([x, jnp.zeros((LANE - n_rows, HEAD_DIM), F32)], axis=0)


def _head_match(n_rows, n_cols, n_heads, n_tok):
    r = lax.broadcasted_iota(I32, (n_rows, n_cols), 0)
    c = lax.broadcasted_iota(I32, (n_rows, n_cols), 1)
    hshift, tshift = int(math.log2(n_heads)), int(math.log2(n_tok))
    same = (lax.shift_right_logical(r, tshift) & (n_heads - 1)) == (c & (n_heads - 1))
    return same, lax.shift_right_logical(c, hshift), r & (n_tok - 1)


def _page_specs(n_pages, page, n_heads, which, n_steps, active_phase):
    def make(slot):
        def index(b, t, pt):
            a = t % n_steps
            if active_phase is not None:
                mine = (t // n_steps) == active_phase
                a = jnp.where(mine, a, 1 if active_phase == 1 else n_steps - 1)
            pos = n_pages - 1 - ((jnp.maximum(a, 1) - 1) * PAGES_PER_STEP + slot)
            return (0, pt[b, pos], 0, which, 0, 0)
        return pl.BlockSpec((None, None, page, None, n_heads, HEAD_DIM), index)
    return [make(s) for s in range(PAGES_PER_STEP)]


def _softmax_stats(s, m_prev, l_prev):
    m_new = jnp.maximum(m_prev, jnp.max(s, axis=-1, keepdims=True))
    l_new = jnp.exp(m_prev - m_new) * l_prev + jnp.sum(jnp.exp(s - m_new), axis=-1, keepdims=True)
    return m_new, l_new


def _fox_sample_kernel(pt_ref, q_ref, nkv_ref, nlf_ref, rowsuf_ref, *refs,
                       n_heads, n_tok, scale, n_steps):
    del pt_ref
    pps = PAGES_PER_STEP
    k_refs, v_refs, lf_refs = refs[:pps], refs[pps:2 * pps], refs[2 * pps:3 * pps]
    o_ref, s_scr, snew_scr, m_scr, l_scr, acc_scr, carry_scr = refs[3 * pps:]
    t = pl.program_id(1)
    n_rows = n_heads * n_tok
    hshift = int(math.log2(n_heads))
    q = q_ref[...].astype(BF16)

    def score(k2d, lfv, is_new, carry):
        n_cols = k2d.shape[0]
        n_chunks = n_cols // LANE
        l1 = lax.broadcasted_iota(I32, (LANE, LANE), 0)
        l2 = lax.broadcasted_iota(I32, (LANE, LANE), 1)
        same_head = (l1 & (n_heads - 1)) == (l2 & (n_heads - 1))
        later_key = lax.shift_right_logical(l1, hshift) > lax.shift_right_logical(l2, hshift)
        within_m = jnp.where(same_head & later_key, 1.0, 0.0).astype(BF16)
        total_m = jnp.where(same_head, 1.0, 0.0).astype(BF16)
        within = _dot_sel(within_m, lfv, sel_first=False)
        rowtot = _dot_sel(total_m, lfv, sel_first=False)
        ridx = lax.broadcasted_iota(I32, lfv.shape, 0)
        rowsuf = jnp.zeros(lfv.shape, F32)
        for a in range(1, lfv.shape[0]):
            rowsuf = rowsuf + jnp.where(ridx < a, rowtot[a:a + 1, :], 0.0)
        suf = within + rowsuf + carry
        carry = carry + jnp.sum(rowtot, axis=0, keepdims=True)
        bias = (jnp.concatenate([suf[a:a + 1, :] for a in range(n_chunks)], axis=1)
                if n_chunks > 1 else suf[0:1, :])
        s = _dot(q, k2d.astype(BF16), NT) * scale + (bias - rowsuf_ref[...])
        same, key, tok = _head_match(n_rows, n_cols, n_heads, n_tok)
        valid = (same & (key <= tok)) if is_new else same
        return jnp.where(valid, s, NEG), carry

    def weights(s, m, inv_l):
        return (jnp.exp(s - m) * inv_l).astype(BF16)

    @pl.when(t == 0)
    def _():
        s, carry = score(_new_rows(nkv_ref, 0, n_rows), nlf_ref[...], True,
                         jnp.zeros(carry_scr.shape, F32))
        snew_scr[...] = s
        m, l = _softmax_stats(s, jnp.full(m_scr.shape, NEG, F32), jnp.zeros(l_scr.shape, F32))
        m_scr[...], l_scr[...], carry_scr[...] = m, l, carry

    @pl.when(jnp.logical_and(t > 0, t < n_steps))
    def _():
        m, l, carry = m_scr[...], l_scr[...], carry_scr[...]
        for i in range(pps):
            s, carry = score(_rows2d(k_refs[i]), lf_refs[i][...], False, carry)
            s_scr[(t - 1) * pps + i] = s
            m, l = _softmax_stats(s, m, l)
        m_scr[...], l_scr[...], carry_scr[...] = m, l, carry

    @pl.when(t == n_steps)
    def _():
        pw = weights(snew_scr[...], m_scr[...], 1.0 / l_scr[...])
        acc_scr[...] = _dot(pw, _new_rows(nkv_ref, 1, n_rows).astype(BF16))

    @pl.when(t > n_steps)
    def _():
        m, inv_l, acc = m_scr[...], 1.0 / l_scr[...], acc_scr[...]
        for i in range(pps):
            pw = weights(s_scr[(t - n_steps - 1) * pps + i], m, inv_l)
            acc = acc + _dot(pw, _rows2d(v_refs[i]).astype(BF16))
        acc_scr[...] = acc

    @pl.when(t == 2 * n_steps - 1)
    def _():
        o_ref[...] = acc_scr[...]


def _sample_rows(x, batch, n_tok, n_heads):
    return (x.reshape(batch, n_tok, n_heads, HEAD_DIM).transpose(0, 2, 1, 3)
            .reshape(batch, n_heads * n_tok, HEAD_DIM))


def _sample_unrows(o, batch, n_tok, n_heads):
    return (o.reshape(batch, n_heads, n_tok, HEAD_DIM).transpose(0, 2, 1, 3)
            .reshape(batch * n_tok, n_heads * HEAD_DIM))


def _fox_sample_call(q, new_kv, new_lf, new_suf, cache_kv, cache_lf, page_table, n_tok):
    batch, n_pages = page_table.shape
    n_layers, n_phys, page, _, n_heads, _ = cache_kv.shape
    pps = PAGES_PER_STEP
    assert n_pages % pps == 0
    n_steps = 1 + n_pages // pps
    n_rows = n_heads * n_tok
    n_cols = page * n_heads
    lf_rows = n_cols // LANE
    q_rows = _sample_rows(q, batch, n_tok, n_heads)
    nkv = new_kv.reshape(batch, n_tok, 2, n_heads, HEAD_DIM)
    nlf = jnp.pad(new_lf.reshape(batch, 1, n_rows), ((0, 0), (0, 7), (0, LANE - n_rows)))
    rowsuf = new_suf.reshape(batch, n_tok, n_heads).transpose(0, 2, 1).reshape(batch, n_rows, 1)
    lf_pages = cache_lf.reshape(n_layers, n_phys, lf_rows, LANE)

    def lf_spec(slot):
        def index(b, t, pt):
            a = jnp.where(t < n_steps, t, n_steps - 1)
            pos = n_pages - 1 - ((jnp.maximum(a, 1) - 1) * pps + slot)
            return (0, pt[b, pos], 0, 0)
        return pl.BlockSpec((None, None, lf_rows, LANE), index)

    grid_spec = pltpu.PrefetchScalarGridSpec(
        num_scalar_prefetch=1,
        grid=(batch, 2 * n_steps),
        in_specs=[
            pl.BlockSpec((None, n_rows, HEAD_DIM), lambda b, t, pt: (b, 0, 0)),
            pl.BlockSpec((None, n_tok, 2, n_heads, HEAD_DIM), lambda b, t, pt: (b, 0, 0, 0, 0)),
            pl.BlockSpec((None, 8, LANE), lambda b, t, pt: (b, 0, 0)),
            pl.BlockSpec((None, n_rows, 1), lambda b, t, pt: (b, 0, 0)),
            *_page_specs(n_pages, page, n_heads, 0, n_steps, 0),
            *_page_specs(n_pages, page, n_heads, 1, n_steps, 1),
            *[lf_spec(s) for s in range(pps)],
        ],
        out_specs=pl.BlockSpec((None, n_rows, HEAD_DIM), lambda b, t, pt: (b, 0, 0)),
        scratch_shapes=[pltpu.VMEM((n_pages, n_rows, n_cols), F32),
                        pltpu.VMEM((n_rows, LANE), F32),
                        pltpu.VMEM((n_rows, 1), F32), pltpu.VMEM((n_rows, 1), F32),
                        pltpu.VMEM((n_rows, HEAD_DIM), F32), pltpu.VMEM((1, LANE), F32)],
    )
    o = pl.pallas_call(
        functools.partial(_fox_sample_kernel, n_heads=n_heads, n_tok=n_tok,
                          scale=HEAD_DIM ** -0.5, n_steps=n_steps),
        grid_spec=grid_spec,
        out_shape=jax.ShapeDtypeStruct((batch, n_rows, HEAD_DIM), F32),
        compiler_params=_cparams(("arbitrary", "arbitrary")),
        name="fox_sample",
    )(page_table, q_rows, nkv, nlf, rowsuf, *([cache_kv] * (2 * pps)), *([lf_pages] * pps))
    return _sample_unrows(o, batch, n_tok, n_heads)


def _diff_sample_kernel(pt_ref, rb_ref, q_ref, nkv_ref, lam_ref, *refs,
                        n_heads, n_tok, page, scale, lam_init, n_steps):
    del pt_ref
    pps = PAGES_PER_STEP
    k_refs, v_refs = refs[:pps], refs[pps:2 * pps]
    o_ref, s_scr, snew_scr, m_scr, l_scr, acc_scr = refs[2 * pps:]
    t = pl.program_id(1)
    n_rows = n_heads * n_tok
    n_cols = page * n_heads
    q2 = q_ref[...].astype(BF16)

    def rel_bias_cols():
        hh = lax.broadcasted_iota(I32, (LANE, n_cols), 0)
        cc = lax.broadcasted_iota(I32, (LANE, n_cols), 1)
        head_sel = jnp.where(hh == (cc & (n_heads - 1)), 1.0, 0.0).astype(BF16)
        return _dot_sel(head_sel, rb_ref[...], sel_first=False)

    def score(k2d, rb_cols, dist0, is_new):
        nc = k2d.shape[0]
        s = _dot(q2, k2d.astype(BF16), NT) * scale
        same, key, tok = _head_match(2 * n_rows, nc, n_heads, n_tok)
        if dist0 is None:
            bias = rb_cols[N_BUCKETS - 1:N_BUCKETS, :nc]
        else:
            bucket = _t5_bucket(dist0 + tok - key)
            bias = jnp.zeros(s.shape, F32)
            for b in range(N_BUCKETS):
                bias = jnp.where(bucket == b, rb_cols[b:b + 1, :nc], bias)
        valid = (same & (key <= tok)) if is_new else same
        return jnp.where(valid, s + bias, NEG)

    def weights(s, m, inv_l, lam):
        pn = jnp.exp(s - m) * inv_l
        return (pn[:n_rows] - lam * pn[n_rows:]).astype(BF16)

    @pl.when(t == 0)
    def _():
        s = score(_new_rows(nkv_ref, 0, n_rows), rel_bias_cols(), 0, True)
        snew_scr[...] = s
        m, l = _softmax_stats(s, jnp.full(m_scr.shape, NEG, F32), jnp.zeros(l_scr.shape, F32))
        m_scr[...], l_scr[...] = m, l

    @pl.when(t == 1)
    def _():
        rb_cols = rel_bias_cols()
        m, l = m_scr[...], l_scr[...]
        for i in range(pps):
            s = score(_rows2d(k_refs[i]), rb_cols, page if i == 0 else None, False)
            s_scr[i] = s
            m, l = _softmax_stats(s, m, l)
        m_scr[...], l_scr[...] = m, l

    @pl.when(jnp.logical_and(t > 1, t < n_steps))
    def _():
        rb_cols = rel_bias_cols()
        m, l = m_scr[...], l_scr[...]
        for i in range(pps):
            s = score(_rows2d(k_refs[i]), rb_cols, None, False)
            s_scr[(t - 1) * pps + i] = s
            m, l = _softmax_stats(s, m, l)
        m_scr[...], l_scr[...] = m, l

    @pl.when(t == n_steps)
    def _():
        pw = weights(snew_scr[...], m_scr[...], 1.0 / l_scr[...], _diff_lambda(lam_ref, lam_init))
        acc_scr[...] = _dot(pw, _new_rows(nkv_ref, 1, n_rows).astype(BF16))

    @pl.when(t > n_steps)
    def _():
        m, inv_l, acc = m_scr[...], 1.0 / l_scr[...], acc_scr[...]
        lam = _diff_lambda(lam_ref, lam_init)
        for i in range(pps):
            pw = weights(s_scr[(t - n_steps - 1) * pps + i], m, inv_l, lam)
            acc = acc + _dot(pw, _rows2d(v_refs[i]).astype(BF16))
        acc_scr[...] = acc

    @pl.when(t == 2 * n_steps - 1)
    def _():
        o = acc_scr[...]
        o_ref[...] = o * lax.rsqrt(jnp.mean(o * o, axis=-1, keepdims=True) + EPS) * (1.0 - lam_init)


def _diff_sample_call(q, new_kv, cache_kv, page_table, rel_bias, lam_diff, n_tok, lam_init):
    batch, n_pages = page_table.shape
    _, _, page, _, n_heads, _ = cache_kv.shape
    pps = PAGES_PER_STEP
    assert n_pages % pps == 0
    assert page >= MAX_DISTANCE
    n_steps = 1 + n_pages // pps
    n_rows = n_heads * n_tok
    n_cols = page * n_heads
    q_rows = _sample_rows(q, batch, n_tok, n_heads)
    lane = jnp.arange(HEAD_DIM) < DIFF_DK
    q2 = jnp.concatenate([jnp.where(lane, q_rows, 0.0), jnp.where(lane, 0.0, q_rows)], axis=1)
    nkv = new_kv.reshape(batch, n_tok, 2, n_heads, HEAD_DIM)
    rb_pad = jnp.pad(rel_bias, ((0, 0), (0, LANE - n_heads)))
    grid_spec = pltpu.PrefetchScalarGridSpec(
        num_scalar_prefetch=1,
        grid=(batch, 2 * n_steps),
        in_specs=[
            pl.BlockSpec(rb_pad.shape, lambda b, t, pt: (0, 0)),
            pl.BlockSpec((None, 2 * n_rows, HEAD_DIM), lambda b, t, pt: (b, 0, 0)),
            pl.BlockSpec((None, n_tok, 2, n_heads, HEAD_DIM), lambda b, t, pt: (b, 0, 0, 0, 0)),
            pl.BlockSpec(lam_diff.shape, lambda b, t, pt: (0, 0)),
            *_page_specs(n_pages, page, n_heads, 0, n_steps, 0),
            *_page_specs(n_pages, page, n_heads, 1, n_steps, 1),
        ],
        out_specs=pl.BlockSpec((None, n_rows, HEAD_DIM), lambda b, t, pt: (b, 0, 0)),
        scratch_shapes=[pltpu.VMEM((n_pages, 2 * n_rows, n_cols), F32),
                        pltpu.VMEM((2 * n_rows, LANE), F32),
                        pltpu.VMEM((2 * n_rows, 1), F32), pltpu.VMEM((2 * n_rows, 1), F32),
                        pltpu.VMEM((n_rows, HEAD_DIM), F32)],
    )
    o = pl.pallas_call(
        functools.partial(_diff_sample_kernel, n_heads=n_heads, n_tok=n_tok, page=page,
                          scale=DIFF_DK ** -0.5, lam_init=lam_init, n_steps=n_steps),
        grid_spec=grid_spec,
        out_shape=jax.ShapeDtypeStruct((batch, n_rows, HEAD_DIM), F32),
        compiler_params=_cparams(("arbitrary", "arbitrary")),
        name="diff_sample",
    )(page_table, rb_pad, q2, nkv, lam_diff, *([cache_kv] * (2 * pps)))
    return _sample_unrows(o, batch, n_tok, n_heads)


def _sb_sample_kernel(pt_ref, q_ref, nkv_ref, gat_ref, sct_ref, *refs, n_heads, n_tok, scale):
    del pt_ref
    pps = PAGES_PER_STEP
    k_refs, v_refs = refs[:pps], refs[pps:2 * pps]
    o_ref, acc_scr, r_scr = refs[2 * pps:]
    t = pl.program_id(1)
    n_rows = n_heads * n_tok
    q = q_ref[...].astype(BF16)

    def attend(k2d, v2d, is_new, acc, r):
        nc = k2d.shape[0]
        a_ = lax.broadcasted_iota(I32, (LANE, LANE), 0)
        b_ = lax.broadcasted_iota(I32, (LANE, LANE), 1)
        later = jnp.where(a_ > b_, 1.0, 0.0).astype(BF16)
        z = _dot(q, k2d.astype(BF16), NT) * scale
        same, key, tok = _head_match(n_rows, nc, n_heads, n_tok)
        valid = (same & (key < tok)) if is_new else same
        ls = _log_sigmoid(z)
        lr = jnp.where(valid, ls - z, 0.0)
        compact = _dot_sel2(lr, gat_ref[:nc, :])
        suf = _dot_sel(later, compact, sel_first=False)
        after = _dot_sel2(suf, sct_ref[:, :nc]) + r
        a = jnp.where(valid, jnp.exp(ls + after), 0.0)
        return (acc + _dot(a.astype(BF16), v2d.astype(BF16)),
                r + jnp.sum(compact, axis=-1, keepdims=True))

    @pl.when(t == 0)
    def _():
        acc, r = attend(_new_rows(nkv_ref, 0, n_rows), _new_rows(nkv_ref, 1, n_rows), True,
                        jnp.zeros(acc_scr.shape, F32), jnp.zeros(r_scr.shape, F32))
        acc_scr[...] = acc
        r_scr[...] = r

    @pl.when(t > 0)
    def _():
        acc, r = acc_scr[...], r_scr[...]
        for i in range(pps):
            acc, r = attend(_rows2d(k_refs[i]), _rows2d(v_refs[i]), False, acc, r)
        acc_scr[...] = acc
        r_scr[...] = r

    @pl.when(t == pl.num_programs(1) - 1)
    def _():
        o_ref[...] = acc_scr[...]


def _sb_sample_call(q, new_kv, cache_kv, page_table, n_tok):
    batch, n_pages = page_table.shape
    _, _, page, _, n_heads, _ = cache_kv.shape
    pps = PAGES_PER_STEP
    assert n_pages % pps == 0 and page <= LANE
    n_steps = 1 + n_pages // pps
    n_rows = n_heads * n_tok
    n_cols = page * n_heads
    q_rows = _sample_rows(q, batch, n_tok, n_heads)
    nkv = new_kv.reshape(batch, n_tok, 2, n_heads, HEAD_DIM)
    key_of_col = jnp.arange(n_cols, dtype=I32) // n_heads
    gat = (key_of_col[:, None] == jnp.arange(LANE, dtype=I32)[None, :]).astype(BF16)
    grid_spec = pltpu.PrefetchScalarGridSpec(
        num_scalar_prefetch=1,
        grid=(batch, n_steps),
        in_specs=[
            pl.BlockSpec((None, n_rows, HEAD_DIM), lambda b, t, pt: (b, 0, 0)),
            pl.BlockSpec((None, n_tok, 2, n_heads, HEAD_DIM), lambda b, t, pt: (b, 0, 0, 0, 0)),
            pl.BlockSpec((n_cols, LANE), lambda b, t, pt: (0, 0)),
            pl.BlockSpec((LANE, n_cols), lambda b, t, pt: (0, 0)),
            *_page_specs(n_pages, page, n_heads, 0, n_steps, None),
            *_page_specs(n_pages, page, n_heads, 1, n_steps, None),
        ],
        out_specs=pl.BlockSpec((None, n_rows, HEAD_DIM), lambda b, t, pt: (b, 0, 0)),
        scratch_shapes=[pltpu.VMEM((n_rows, HEAD_DIM), F32), pltpu.VMEM((n_rows, 1), F32)],
    )
    o = pl.pallas_call(
        functools.partial(_sb_sample_kernel, n_heads=n_heads, n_tok=n_tok, scale=HEAD_DIM ** -0.5),
        grid_spec=grid_spec,
        out_shape=jax.ShapeDtypeStruct((batch, n_rows, HEAD_DIM), F32),
        compiler_params=_cparams(("arbitrary", "arbitrary")),
        name="sb_sample",
    )(page_table, q_rows, nkv, gat, gat.T, *([cache_kv] * (2 * pps)))
    return _sample_unrows(o, batch, n_tok, n_heads)


TM_PROMPT = 1024
TQ_PROMPT = 512
TM_TOKEN = 128
TM_GROUP = 1024
GROUP_SUB = 256


def kernel(x_prompt, x_sample, cache_fox_kv, cache_fox_logf, cache_diff_kv, cache_sb_kv, page_table,
           c_prompt, c_sample, w_ada, b_ada, g_norm, w_in_even, b_forget, g_q_fox, g_k_fox,
           g_q_diff, g_k_diff, lam_diff, rel_bias, w_out_even, w_gate, w_up, w_down,
           w_in_odd, w_out_odd, w_router, b_router, w_gate_exp, w_up_exp, w_down_exp):
    bp, seq, d = x_prompt.shape
    bs, n_tok, _ = x_sample.shape
    h_fox = cache_fox_kv.shape[4]
    h_diff = cache_diff_kv.shape[4]
    h_sb = cache_sb_kv.shape[4]
    fox_w, diff_w, sb_w = h_fox * HEAD_DIM, h_diff * HEAD_DIM, h_sb * HEAD_DIM
    mp, ms = bp * seq, bs * n_tok

    n_c = bp + bs
    c_all = jnp.pad(jnp.concatenate([c_prompt, c_sample], axis=0), ((0, -n_c % 8), (0, 0)))
    mods = _ada_call(c_all, w_ada, b_ada)

    def mod6(layer, lo, hi):
        return [mods[layer, lo:hi, k * d:(k + 1) * d] for k in range(6)]

    xp = x_prompt.reshape(mp, d)
    xs = x_sample.reshape(ms, d)

    lam_init0 = 0.8 - 0.6 * math.exp(-0.3 * 0)
    w_in0 = w_in_even[0]
    w_fg = w_in0[:, 3 * fox_w:3 * fox_w + h_fox]
    w_diff_in = w_in0[:, 3 * fox_w + h_fox:]
    gain_fox = jnp.concatenate([jnp.tile(g_q_fox[0], h_fox), jnp.tile(g_k_fox[0], h_fox),
                                jnp.ones((fox_w,), F32)]).reshape(1, 3 * fox_w)
    gain_diff = jnp.concatenate([jnp.tile(g_q_diff[0], 2 * h_diff), jnp.tile(g_k_diff[0], 2 * h_diff),
                                 jnp.ones((diff_w,), F32)]).reshape(1, 3 * diff_w)
    tt = _bias_tile_call(rel_bias, TQ_PROMPT)

    sh1, sc1, g1, sh2, sc2, g2 = mod6(0, 0, bp)
    hp = _normmod_call(xp, g_norm[0, 0], sc1, sh1, BF16, seq, TM_PROMPT)
    fox_qkv, fox_kv_p = _inproj_call(hp, w_in0, 0, 3 * fox_w, gain_fox, HEAD_DIM, True, TM_PROMPT)
    diff_qkv, diff_kv_p = _inproj_call(hp, w_diff_in, 0, 3 * diff_w, gain_diff, DIFF_DK, True,
                                       TM_PROMPT)
    logf_p, ck_p, _ = _logf_call(hp, w_fg, b_forget[0], seq, TQ_PROMPT)
    ck_rows = ck_p.reshape(bp, seq, h_fox).transpose(0, 2, 1)
    o_fox = _fox_prompt_call(fox_qkv, jnp.repeat(ck_p, HEAD_DIM, axis=1), ck_rows, bp, seq, h_fox,
                             TQ_PROMPT)
    o_diff = _diff_prompt_call(diff_qkv, tt, lam_diff[0], bp, seq, h_diff, TQ_PROMPT, lam_init0)
    xp = _outproj_call([o_fox, o_diff], w_out_even[0], xp, g1, seq, TM_PROMPT)
    hp = _normmod_call(xp, g_norm[0, 1], sc2, sh2, BF16, seq, TM_PROMPT)
    xp = _swiglu_call(hp, w_gate[0], w_up[0], w_down[0], xp, g2, seq, TM_PROMPT)

    sh1, sc1, g1, sh2, sc2, g2 = mod6(0, bp, bp + bs)
    hs_ = _normmod_call(xs, g_norm[0, 0], sc1, sh1, F32, n_tok, ms)
    fox_s = _inproj_call(hs_, w_in0, 0, 3 * fox_w, gain_fox, HEAD_DIM, False, ms)
    diff_s = _inproj_call(hs_, w_diff_in, 0, 3 * diff_w, gain_diff, DIFF_DK, False, ms)
    logf_s, _, suf_s = _logf_call(hs_, w_fg, b_forget[0], n_tok, ms)
    fox_kv_s, diff_kv_s = fox_s[:, fox_w:], diff_s[:, diff_w:]
    o_fox = _fox_sample_call(fox_s[:, :fox_w], fox_kv_s, logf_s, suf_s, cache_fox_kv, cache_fox_logf,
                             page_table, n_tok)
    o_diff = _diff_sample_call(diff_s[:, :diff_w], diff_kv_s, cache_diff_kv, page_table, rel_bias,
                               lam_diff[0], n_tok, lam_init0)
    xs = _outproj_call([o_fox, o_diff], w_out_even[0], xs, g1, n_tok, ms)
    hs_ = _normmod_call(xs, g_norm[0, 1], sc2, sh2, F32, n_tok, ms)
    xs = _swiglu_call(hs_, w_gate[0], w_up[0], w_down[0], xs, g2, n_tok, ms)

    w_in1 = w_in_odd[0]
    sh1, sc1, g1, sh2p, sc2p, g2p = mod6(1, 0, bp)
    hp = _normmod_call(xp, g_norm[1, 0], sc1, sh1, BF16, seq, TM_PROMPT)
    sb_qkv, sb_kv_p = _inproj_call(hp, w_in1, 0, 3 * sb_w, None, HEAD_DIM, True, TM_PROMPT)
    o_sb = _sb_prompt_call(sb_qkv, bp, seq, h_sb, TQ_PROMPT, 4)
    xp = _outproj_call([o_sb], w_out_odd[0], xp, g1, seq, TM_PROMPT)
    hp2 = _normmod_call(xp, g_norm[1, 1], sc2p, sh2p, F32, seq, TM_PROMPT)

    sh1, sc1, g1, sh2s, sc2s, g2s = mod6(1, bp, bp + bs)
    hs_ = _normmod_call(xs, g_norm[1, 0], sc1, sh1, F32, n_tok, ms)
    sb_s = _inproj_call(hs_, w_in1, 0, 3 * sb_w, None, HEAD_DIM, False, ms)
    sb_kv_s = sb_s[:, sb_w:]
    o_sb = _sb_sample_call(sb_s[:, :sb_w], sb_kv_s, cache_sb_kv, page_table, n_tok)
    xs = _outproj_call([o_sb], w_out_odd[0], xs, g1, n_tok, ms)
    hs2 = _normmod_call(xs, g_norm[1, 1], sc2s, sh2s, F32, n_tok, ms)

    yp, ys = _moe([hp2, hs2], [xp, xs], [g2p, g2s], [seq, n_tok], w_router[0], b_router[0],
                  w_gate_exp[0], w_up_exp[0], w_down_exp[0], TM_TOKEN, TM_GROUP)

    return (
        yp.reshape(bp, seq, d),
        ys.reshape(bs, n_tok, d),
        fox_kv_p.reshape(1, bp, seq, 2, h_fox, HEAD_DIM),
        logf_p.reshape(1, bp, seq, h_fox),
        diff_kv_p.reshape(1, bp, seq, 2, h_diff, HEAD_DIM),
        sb_kv_p.reshape(1, bp, seq, 2, h_sb, HEAD_DIM),
        fox_kv_s.reshape(1, bs, n_tok, 2, h_fox, HEAD_DIM),
        logf_s.reshape(1, bs, n_tok, h_fox),
        diff_kv_s.reshape(1, bs, n_tok, 2, h_diff, HEAD_DIM),
        sb_kv_s.reshape(1, bs, n_tok, 2, h_sb, HEAD_DIM),
    )
```

```python
import functools
import math

import numpy as np
import jax
import jax.numpy as jnp
from jax import lax
from jax.experimental import pallas as pl
from jax.experimental.pallas import tpu as pltpu

F32 = jnp.float32
BF16 = jnp.bfloat16
I32 = jnp.int32

EPS = 1e-6
NEG = -1e30
LANE = 128
HEAD_DIM = 128
DIFF_DK = 64
N_BUCKETS = 32
MAX_DISTANCE = 128
N_EXPERTS = 8
VMEM_LIMIT = 56 * 1024 * 1024

NN = ((1,), (0,))
NT = ((1,), (1,))


def _cparams(sem):
    return pltpu.CompilerParams(dimension_semantics=sem, vmem_limit_bytes=VMEM_LIMIT)


def _dot(a, b, dims=NN):
    return lax.dot_general(a, b, (dims, ((), ())), preferred_element_type=F32)


def _split2(a):
    hi = a.astype(BF16)
    lo = (a - hi.astype(F32)).astype(BF16)
    return hi, lo


def _split3(a):
    hi = a.astype(BF16)
    r = a - hi.astype(F32)
    mid = r.astype(BF16)
    lo = (r - mid.astype(F32)).astype(BF16)
    return hi, mid, lo


def _dot_sel(sel_bf16, x, dims=NN, sel_first=True):
    acc = None
    for part in _split3(x.astype(F32)):
        t = _dot(sel_bf16, part, dims) if sel_first else _dot(part, sel_bf16, dims)
        acc = t if acc is None else acc + t
    return acc


def _dot_sel2(x, sel_bf16):
    n = x.shape[0]
    hi, lo = _split2(x)
    both = _dot(jnp.concatenate([hi, lo], axis=0), sel_bf16)
    return both[:n] + both[n:]


def _mm(a, w):
    return _dot(a.astype(BF16), w.astype(BF16))


def _log_sigmoid(x):
    return jnp.minimum(x, 0.0) - jnp.log1p(jnp.exp(-jnp.abs(x)))


def _log_sigmoid_exponent(x):
    return jnp.minimum(x, 0.0) - jnp.log(1.0 + jnp.exp(-jnp.abs(x)))


def _silu(x):
    return x / (1.0 + jnp.exp(-x))


def _ada_kernel(c_ref, w_ref, b_ref, o_ref):
    o_ref[...] = _mm(_silu(c_ref[...]), w_ref[...]) + b_ref[...]


def _ada_call(c_all, w_ada, b_ada, tn=1024):
    n_layers, d, n = w_ada.shape
    mc = c_all.shape[0]
    return pl.pallas_call(
        _ada_kernel,
        grid=(n_layers, n // tn),
        in_specs=[
            pl.BlockSpec((mc, d), lambda l, j: (0, 0)),
            pl.BlockSpec((None, d, tn), lambda l, j: (l, 0, j)),
            pl.BlockSpec((None, 1, tn), lambda l, j: (l, 0, j)),
        ],
        out_specs=pl.BlockSpec((None, mc, tn), lambda l, j: (l, 0, j)),
        out_shape=jax.ShapeDtypeStruct((n_layers, mc, n), F32),
        compiler_params=_cparams(("arbitrary", "arbitrary")),
        name="ada",
    )(c_all, w_ada, b_ada.reshape(n_layers, 1, n))


def _normmod_kernel(x_ref, g_ref, sc_ref, sh_ref, o_ref):
    x = x_ref[...]
    y = x * lax.rsqrt(jnp.mean(x * x, axis=-1, keepdims=True) + EPS)
    y = y * g_ref[...]
    o_ref[...] = (y * (1.0 + sc_ref[...]) + sh_ref[...]).astype(o_ref.dtype)


def _normmod_call(x, g, sc, sh, out_dtype, rows_per_batch, tm):
    m, d = x.shape
    if rows_per_batch % tm == 0:
        per = rows_per_batch // tm
        sc_in, sh_in = sc.reshape(-1, 1, d), sh.reshape(-1, 1, d)
        mod_spec = pl.BlockSpec((None, 1, d), lambda i: (i // per, 0, 0))
    else:
        sc_in = jnp.repeat(sc, rows_per_batch, axis=0)
        sh_in = jnp.repeat(sh, rows_per_batch, axis=0)
        mod_spec = pl.BlockSpec((tm, d), lambda i: (i, 0))
    return pl.pallas_call(
        _normmod_kernel,
        grid=(m // tm,),
        in_specs=[
            pl.BlockSpec((tm, d), lambda i: (i, 0)),
            pl.BlockSpec((1, d), lambda i: (0, 0)),
            mod_spec,
            mod_spec,
        ],
        out_specs=pl.BlockSpec((tm, d), lambda i: (i, 0)),
        out_shape=jax.ShapeDtypeStruct((m, d), out_dtype),
        compiler_params=_cparams(("arbitrary",)),
        name="normmod",
    )(x, g.reshape(1, d), sc_in, sh_in)


def _group_rms(y, gain, group):
    outs = []
    lane = lax.broadcasted_iota(I32, (1, LANE), 1)
    for c in range(y.shape[1] // LANE):
        blk = y[:, c * LANE:(c + 1) * LANE]
        sq = blk * blk
        if group == LANE:
            ms = jnp.mean(sq, axis=-1, keepdims=True)
        else:
            lo_half = lane < group
            s0 = jnp.sum(jnp.where(lo_half, sq, 0.0), axis=-1, keepdims=True)
            s1 = jnp.sum(jnp.where(lo_half, 0.0, sq), axis=-1, keepdims=True)
            ms = jnp.where(lo_half, s0, s1) * (1.0 / group)
        outs.append(blk * lax.rsqrt(ms + EPS) * gain[:, c * LANE:(c + 1) * LANE])
    return jnp.concatenate(outs, axis=-1) if len(outs) > 1 else outs[0]


def _inproj_kernel(a_ref, w_ref, gain_ref, *out_refs, group, n_norm_tiles, n_q_tiles,
                   with_bf16):
    j = pl.program_id(1)
    y = _mm(a_ref[...], w_ref[...])
    if with_bf16:
        qkv_ref, kv_ref = out_refs
    else:
        (qkv_ref,), kv_ref = out_refs, None

    def store(val):
        qkv_ref[...] = val.astype(qkv_ref.dtype)
        if kv_ref is not None:
            @pl.when(j >= n_q_tiles)
            def _():
                kv_ref[...] = val

    if n_norm_tiles == 0:
        store(y)
    else:
        @pl.when(j < n_norm_tiles)
        def _():
            store(_group_rms(y, gain_ref[...], group))

        @pl.when(j >= n_norm_tiles)
        def _():
            store(y)


def _inproj_call(h, w, col_off, n_cols, gain, group, with_bf16, tm, tn=512):
    m, k = h.shape
    assert col_off % tn == 0 and n_cols % (3 * tn) == 0
    n_tiles = n_cols // tn
    n_q_tiles = n_tiles // 3
    n_norm_tiles = 0 if gain is None else 2 * n_q_tiles
    if gain is None:
        gain = jnp.ones((1, n_cols), F32)
    off = col_off // tn
    kern = functools.partial(_inproj_kernel, group=group, n_norm_tiles=n_norm_tiles,
                             n_q_tiles=n_q_tiles, with_bf16=with_bf16)
    if with_bf16:
        out_specs = [pl.BlockSpec((tm, tn), lambda i, j: (i, j)),
                     pl.BlockSpec((tm, tn), lambda i, j: (i, jnp.maximum(j - n_q_tiles, 0)))]
        out_shape = [jax.ShapeDtypeStruct((m, n_cols), BF16),
                     jax.ShapeDtypeStruct((m, n_cols - n_cols // 3), F32)]
    else:
        out_specs = [pl.BlockSpec((tm, tn), lambda i, j: (i, j))]
        out_shape = [jax.ShapeDtypeStruct((m, n_cols), F32)]
    res = pl.pallas_call(
        kern,
        grid=(m // tm, n_tiles),
        in_specs=[
            pl.BlockSpec((tm, k), lambda i, j: (i, 0)),
            pl.BlockSpec((k, tn), lambda i, j: (0, j + off)),
            pl.BlockSpec((1, tn), lambda i, j: (0, j)),
        ],
        out_specs=out_specs,
        out_shape=out_shape,
        compiler_params=_cparams(("arbitrary", "arbitrary")),
        name="inproj",
    )(h, w, gain)
    return res if with_bf16 else res[0]


def _logf_kernel(a_ref, w_ref, b_ref, lf_ref, ck_ref, suf_ref, carry_ref, *, seg):
    i = pl.program_id(0)
    tm = a_ref.shape[0]
    lf = _log_sigmoid(_mm(a_ref[...], w_ref[...]) + b_ref[...])
    lf_ref[...] = lf
    shift = int(math.log2(min(seg, tm)))
    r = lax.broadcasted_iota(I32, (tm, tm), 0)
    c = lax.broadcasted_iota(I32, (tm, tm), 1)
    same = lax.shift_right_logical(r, shift) == lax.shift_right_logical(c, shift)
    lower = jnp.where(same & (c <= r), 1.0, 0.0).astype(BF16)
    upper = jnp.where(same & (c > r), 1.0, 0.0).astype(BF16)
    pre = _dot_sel(lower, lf)
    suf_ref[...] = _dot_sel(upper, lf)
    if seg > tm:
        per = seg // tm

        @pl.when(i % per == 0)
        def _():
            carry_ref[...] = jnp.zeros_like(carry_ref)

        pre = pre + carry_ref[...]
        carry_ref[...] = pre[tm - 1:tm, :]
    ck_ref[...] = pre


def _logf_call(h, w_fg, b_f, seg, tm):
    m, k = h.shape
    nh = w_fg.shape[1]
    w_pad = jnp.pad(w_fg, ((0, 0), (0, LANE - nh)))
    b_pad = jnp.pad(b_f.reshape(1, nh), ((0, 0), (0, LANE - nh)))
    spec = pl.BlockSpec((tm, LANE), lambda i: (i, 0))
    shp = jax.ShapeDtypeStruct((m, LANE), F32)
    lf, ck, suf = pl.pallas_call(
        functools.partial(_logf_kernel, seg=seg),
        grid=(m // tm,),
        in_specs=[
            pl.BlockSpec((tm, k), lambda i: (i, 0)),
            pl.BlockSpec((k, LANE), lambda i: (0, 0)),
            pl.BlockSpec((1, LANE), lambda i: (0, 0)),
        ],
        out_specs=[spec, spec, spec],
        out_shape=[shp, shp, shp],
        scratch_shapes=[pltpu.VMEM((1, LANE), F32)],
        compiler_params=_cparams(("arbitrary",)),
        name="logf",
    )(h, w_pad, b_pad)
    return lf[:, :nh], ck[:, :nh], suf[:, :nh]


def _lane_tile(x, width):
    reps = width // LANE
    return jnp.concatenate([x] * reps, axis=1) if reps > 1 else x


def _tri_pairs(n, descending):
    qi, kj = [], []
    for a in range(n):
        ks = range(a, -1, -1) if descending else range(a + 1)
        for b in ks:
            qi.append(a)
            kj.append(b)
    return jnp.asarray(qi, I32), jnp.asarray(kj, I32)


def _fox_prompt_kernel(qi_tab, kj_tab, q_ref, k_ref, v_ref, cq_ref, ck_ref, o_ref,
                       m_scr, l_scr, acc_scr, *, n_heads, scale):
    p = pl.program_id(1)
    qi, kj = qi_tab[p], kj_tab[p]
    tq, tk = q_ref.shape[0], k_ref.shape[0]

    @pl.when(kj == 0)
    def _():
        m_scr[...] = jnp.full_like(m_scr, NEG)
        l_scr[...] = jnp.zeros_like(l_scr)
        acc_scr[...] = jnp.zeros_like(acc_scr)

    def step(masked):
        if masked:
            causal = (lax.broadcasted_iota(I32, (tq, tk), 1)
                      <= lax.broadcasted_iota(I32, (tq, tk), 0))
        for h in range(n_heads):
            hs = slice(h * HEAD_DIM, (h + 1) * HEAD_DIM)
            s = _dot(q_ref[:, hs], k_ref[:, hs], NT) * scale
            s = s + (_lane_tile(cq_ref[:, hs], tk) - ck_ref[h:h + 1, :])
            if masked:
                s = jnp.where(causal, s, NEG)
            m_prev = m_scr[h]
            m_new = jnp.maximum(m_prev, jnp.max(s, axis=-1, keepdims=True))
            alpha = jnp.exp(m_prev - m_new)
            pr = jnp.exp(s - _lane_tile(m_new, tk))
            l_scr[h] = alpha * l_scr[h] + jnp.sum(pr, axis=-1, keepdims=True)
            acc_scr[:, hs] = alpha * acc_scr[:, hs] + _dot(pr.astype(BF16), v_ref[:, hs])
            m_scr[h] = m_new

    @pl.when(kj < qi)
    def _():
        step(False)

    @pl.when(kj == qi)
    def _():
        step(True)
        for h in range(n_heads):
            hs = slice(h * HEAD_DIM, (h + 1) * HEAD_DIM)
            o_ref[:, hs] = (acc_scr[:, hs] / l_scr[h]).astype(o_ref.dtype)


def _fox_prompt_call(qkv, cq, ck_rows, batch, seq, n_heads, tq):
    w = n_heads * HEAD_DIM
    nq = seq // tq
    qi_tab, kj_tab = _tri_pairs(nq, descending=False)
    grid_spec = pltpu.PrefetchScalarGridSpec(
        num_scalar_prefetch=2,
        grid=(batch, int(qi_tab.shape[0])),
        in_specs=[
            pl.BlockSpec((tq, w), lambda b, p, qt, kt: (b * nq + qt[p], 0)),
            pl.BlockSpec((tq, w), lambda b, p, qt, kt: (b * nq + kt[p], 1)),
            pl.BlockSpec((tq, w), lambda b, p, qt, kt: (b * nq + kt[p], 2)),
            pl.BlockSpec((tq, w), lambda b, p, qt, kt: (b * nq + qt[p], 0)),
            pl.BlockSpec((None, n_heads, tq), lambda b, p, qt, kt: (b, 0, kt[p])),
        ],
        out_specs=pl.BlockSpec((tq, w), lambda b, p, qt, kt: (b * nq + qt[p], 0)),
        scratch_shapes=[pltpu.VMEM((n_heads, tq, LANE), F32), pltpu.VMEM((n_heads, tq, LANE), F32),
                        pltpu.VMEM((tq, w), F32)],
    )
    return pl.pallas_call(
        functools.partial(_fox_prompt_kernel, n_heads=n_heads, scale=HEAD_DIM ** -0.5),
        grid_spec=grid_spec,
        out_shape=jax.ShapeDtypeStruct((batch * seq, w), BF16),
        compiler_params=_cparams(("arbitrary", "arbitrary")),
        name="fox_prompt",
    )(qi_tab, kj_tab, qkv, qkv, qkv, cq, ck_rows)


def _t5_bucket(n):
    n = jnp.maximum(n, 0)
    exact = N_BUCKETS // 2
    nf = jnp.maximum(n, 1).astype(F32)
    large = exact + (jnp.log(nf / exact) / math.log(MAX_DISTANCE / exact)
                     * (N_BUCKETS - exact)).astype(I32)
    return jnp.where(n < exact, n, jnp.minimum(large, N_BUCKETS - 1))


def _bias_tile_kernel(rb_ref, o_ref, *, t):
    i = lax.broadcasted_iota(I32, (t, t), 0)
    j = lax.broadcasted_iota(I32, (t, t), 1)
    d = i - j
    bucket = _t5_bucket(jnp.where(d < 0, d + t, d))
    h = pl.program_id(0)
    acc = jnp.zeros((t, t), F32)
    for b in range(N_BUCKETS):
        acc = jnp.where(bucket == b, rb_ref[b, h], acc)
    o_ref[...] = acc


def _bias_tile_call(rel_bias, t):
    n_heads = rel_bias.shape[1]
    return pl.pallas_call(
        functools.partial(_bias_tile_kernel, t=t),
        grid=(n_heads,),
        in_specs=[pl.BlockSpec(memory_space=pltpu.SMEM)],
        out_specs=pl.BlockSpec((None, t, t), lambda h: (h, 0, 0)),
        out_shape=jax.ShapeDtypeStruct((n_heads, t, t), F32),
        compiler_params=_cparams(("arbitrary",)),
        name="bias_tile",
    )(rel_bias)


def _diff_lambda(lam_ref, lam_init):
    lp = lam_ref[...]
    a = jnp.sum(lp[0:1, :] * lp[1:2, :], axis=-1, keepdims=True)
    b = jnp.sum(lp[2:3, :] * lp[3:4, :], axis=-1, keepdims=True)
    return jnp.exp(a) - jnp.exp(b) + lam_init


def _diff_prompt_kernel(qi_tab, kj_tab, q_ref, k_ref, v_ref, tt_ref, lam_ref, o_ref,
                        m_scr, l_scr, acc_scr, *, n_heads, scale, lam_init):
    p = pl.program_id(1)
    qi, kj = qi_tab[p], kj_tab[p]
    tq, tk = q_ref.shape[0], k_ref.shape[0]

    @pl.when(kj == 0)
    def _():
        m_scr[...] = jnp.full_like(m_scr, NEG)
        l_scr[...] = jnp.zeros_like(l_scr)
        acc_scr[...] = jnp.zeros_like(acc_scr)

    def step(diag):
        row = lax.broadcasted_iota(I32, (tq, tk), 0)
        col = lax.broadcasted_iota(I32, (tq, tk), 1)
        lane = lax.broadcasted_iota(I32, (1, HEAD_DIM), 1)
        first_half = lane < DIFF_DK
        if diag:
            causal = col <= row
        else:
            near = col > row + jnp.where(kj == qi - 1, 0, tk)
        for h in range(n_heads):
            hs = slice(h * HEAD_DIM, (h + 1) * HEAD_DIM)
            q = q_ref[:, hs]
            zero = jnp.zeros_like(q)
            q2 = jnp.concatenate([jnp.where(first_half, q, zero),
                                  jnp.where(first_half, zero, q)], axis=0)
            s = _dot(q2, k_ref[:, hs], NT) * scale
            tt = tt_ref[h]
            if diag:
                bias = tt
            else:
                bias = jnp.where(near, tt, tt_ref[h, tq - 1:tq, 0:1])
            for c in range(2):
                rs = slice(c * tq, (c + 1) * tq)
                idx = 2 * h + c
                sc = s[rs] + bias
                if diag:
                    sc = jnp.where(causal, sc, NEG)
                m_prev = m_scr[idx]
                m_new = jnp.maximum(m_prev, jnp.max(sc, axis=-1, keepdims=True))
                alpha = jnp.exp(m_prev - m_new)
                pr = jnp.exp(sc - _lane_tile(m_new, tk))
                l_scr[idx] = alpha * l_scr[idx] + jnp.sum(pr, axis=-1, keepdims=True)
                acc_scr[c, :, hs] = alpha * acc_scr[c, :, hs] + _dot(pr.astype(BF16), v_ref[:, hs])
                m_scr[idx] = m_new

    @pl.when(kj < qi)
    def _():
        step(False)

    @pl.when(kj == qi)
    def _():
        step(True)
        lam = _diff_lambda(lam_ref, lam_init)
        for h in range(n_heads):
            hs = slice(h * HEAD_DIM, (h + 1) * HEAD_DIM)
            o0 = acc_scr[0, :, hs] / l_scr[2 * h]
            o1 = acc_scr[1, :, hs] / l_scr[2 * h + 1]
            o = o0 - lam * o1
            o = o * lax.rsqrt(jnp.mean(o * o, axis=-1, keepdims=True) + EPS) * (1.0 - lam_init)
            o_ref[:, hs] = o.astype(o_ref.dtype)


def _diff_prompt_call(qkv, tt, lam_diff, batch, seq, n_heads, tq, lam_init):
    w = n_heads * HEAD_DIM
    nq = seq // tq
    qi_tab, kj_tab = _tri_pairs(nq, descending=False)
    grid_spec = pltpu.PrefetchScalarGridSpec(
        num_scalar_prefetch=2,
        grid=(batch, int(qi_tab.shape[0])),
        in_specs=[
            pl.BlockSpec((tq, w), lambda b, p, qt, kt: (b * nq + qt[p], 0)),
            pl.BlockSpec((tq, w), lambda b, p, qt, kt: (b * nq + kt[p], 1)),
            pl.BlockSpec((tq, w), lambda b, p, qt, kt: (b * nq + kt[p], 2)),
            pl.BlockSpec((n_heads, tq, tq), lambda b, p, qt, kt: (0, 0, 0)),
            pl.BlockSpec(lam_diff.shape, lambda b, p, qt, kt: (0, 0)),
        ],
        out_specs=pl.BlockSpec((tq, w), lambda b, p, qt, kt: (b * nq + qt[p], 0)),
        scratch_shapes=[pltpu.VMEM((2 * n_heads, tq, LANE), F32),
                        pltpu.VMEM((2 * n_heads, tq, LANE), F32),
                        pltpu.VMEM((2, tq, w), F32)],
    )
    return pl.pallas_call(
        functools.partial(_diff_prompt_kernel, n_heads=n_heads, scale=DIFF_DK ** -0.5,
                          lam_init=lam_init),
        grid_spec=grid_spec,
        out_shape=jax.ShapeDtypeStruct((batch * seq, w), BF16),
        compiler_params=_cparams(("arbitrary", "arbitrary")),
        name="diff_prompt",
    )(qi_tab, kj_tab, qkv, qkv, qkv, tt, lam_diff)


def _suffix_matrix():
    l = lax.broadcasted_iota(I32, (LANE, 2 * LANE), 0)
    j = lax.broadcasted_iota(I32, (LANE, 2 * LANE), 1)
    return jnp.where((l > j) | (j >= LANE), 1.0, 0.0).astype(BF16)


def _sb_tile(z, valid, v, r, u2):
    tk = z.shape[1]
    chunks = [None] * (tk // LANE)
    for c in reversed(range(tk // LANE)):
        zc = z[:, c * LANE:(c + 1) * LANE]
        ls = _log_sigmoid_exponent(zc)
        lr = ls - zc
        if valid is not None:
            vc = valid[:, c * LANE:(c + 1) * LANE]
            lr = jnp.where(vc, lr, 0.0)
        hi, lo = _split2(lr)
        s2 = _dot(hi, u2) + _dot(lo, u2)
        a = jnp.exp(ls + (s2[:, :LANE] + r))
        if valid is not None:
            a = jnp.where(vc, a, 0.0)
        chunks[c] = a.astype(BF16)
        r = r + s2[:, LANE:]
    a_full = jnp.concatenate(chunks, axis=-1) if len(chunks) > 1 else chunks[0]
    return _dot(a_full, v), r


def _sb_prompt_kernel(qi_tab, kj_tab, q_ref, k_ref, v_ref, o_ref, r_scr, acc_scr, *,
                      n_heads, scale):
    p = pl.program_id(2)
    qi, kj = qi_tab[p], kj_tab[p]
    tq, tk = q_ref.shape[0], k_ref.shape[0]
    u2 = _suffix_matrix()

    def step(diag):
        valid = None
        if diag:
            valid = (lax.broadcasted_iota(I32, (tq, tk), 1)
                     < lax.broadcasted_iota(I32, (tq, tk), 0))
        for h in range(n_heads):
            hs = slice(h * HEAD_DIM, (h + 1) * HEAD_DIM)
            z = _dot(q_ref[:, hs], k_ref[:, hs], NT) * scale
            if diag:
                r0 = jnp.zeros((tq, LANE), F32)
                pv, r1 = _sb_tile(z, valid, v_ref[:, hs], r0, u2)
                acc_scr[:, hs] = pv
            else:
                pv, r1 = _sb_tile(z, None, v_ref[:, hs], r_scr[h], u2)
                acc_scr[:, hs] = acc_scr[:, hs] + pv
            r_scr[h] = r1

    @pl.when(kj == qi)
    def _():
        step(True)

    @pl.when(kj < qi)
    def _():
        step(False)

    @pl.when(kj == 0)
    def _():
        o_ref[...] = acc_scr[...].astype(o_ref.dtype)


def _sb_prompt_call(qkv, batch, seq, n_heads, tq, heads_per_step):
    n_groups = n_heads // heads_per_step
    w = heads_per_step * HEAD_DIM
    nq = seq // tq
    qi_tab, kj_tab = _tri_pairs(nq, descending=True)
    grid_spec = pltpu.PrefetchScalarGridSpec(
        num_scalar_prefetch=2,
        grid=(batch, n_groups, int(qi_tab.shape[0])),
        in_specs=[
            pl.BlockSpec((tq, w), lambda b, g, p, qt, kt: (b * nq + qt[p], g)),
            pl.BlockSpec((tq, w), lambda b, g, p, qt, kt: (b * nq + kt[p], n_groups + g)),
            pl.BlockSpec((tq, w), lambda b, g, p, qt, kt: (b * nq + kt[p], 2 * n_groups + g)),
        ],
        out_specs=pl.BlockSpec((tq, w), lambda b, g, p, qt, kt: (b * nq + qt[p], g)),
        scratch_shapes=[pltpu.VMEM((heads_per_step, tq, LANE), F32), pltpu.VMEM((tq, w), F32)],
    )
    return pl.pallas_call(
        functools.partial(_sb_prompt_kernel, n_heads=heads_per_step, scale=HEAD_DIM ** -0.5),
        grid_spec=grid_spec,
        out_shape=jax.ShapeDtypeStruct((batch * seq, n_heads * HEAD_DIM), BF16),
        compiler_params=_cparams(("arbitrary", "arbitrary", "arbitrary")),
        name="sb_prompt",
    )(qi_tab, kj_tab, qkv, qkv, qkv)


def _outproj_kernel(*refs, n_parts):
    a_refs = refs[:n_parts]
    w_refs = refs[n_parts:2 * n_parts]
    x_ref, g_ref, o_ref = refs[2 * n_parts:]
    y = None
    for a_ref, w_ref in zip(a_refs, w_refs):
        t = _mm(a_ref[...], w_ref[...])
        y = t if y is None else y + t
    o_ref[...] = x_ref[...] + g_ref[...] * y


def _gate_spec(g, rows_per_batch, tm, tn, col_tiled):
    d = g.shape[1]
    col = (lambda j: j) if col_tiled else (lambda j: 0)
    if rows_per_batch % tm == 0:
        per = rows_per_batch // tm
        return g.reshape(-1, 1, d), pl.BlockSpec((None, 1, tn), lambda i, j: (i // per, 0, col(j)))
    return (jnp.repeat(g, rows_per_batch, axis=0),
            pl.BlockSpec((tm, tn), lambda i, j: (i, col(j))))


def _outproj_call(parts, w, x, g, rows_per_batch, tm, tn=512):
    m, d = x.shape
    kp = parts[0].shape[1]
    n_parts = len(parts)
    g_in, g_spec = _gate_spec(g, rows_per_batch, tm, tn, True)
    in_specs = [pl.BlockSpec((tm, kp), lambda i, j: (i, 0)) for _ in parts]
    in_specs += [pl.BlockSpec((kp, tn), functools.partial(lambda i, j, c: (c, j), c=c))
                 for c in range(n_parts)]
    in_specs += [pl.BlockSpec((tm, tn), lambda i, j: (i, j)), g_spec]
    return pl.pallas_call(
        functools.partial(_outproj_kernel, n_parts=n_parts),
        grid=(m // tm, d // tn),
        in_specs=in_specs,
        out_specs=pl.BlockSpec((tm, tn), lambda i, j: (i, j)),
        out_shape=jax.ShapeDtypeStruct((m, d), F32),
        compiler_params=_cparams(("arbitrary", "arbitrary")),
        name="outproj",
    )(*parts, *([w] * n_parts), x, g_in)


def _swiglu_kernel(h_ref, wg_ref, wu_ref, wd_ref, x_ref, g_ref, o_ref):
    j = pl.program_id(1)
    h = h_ref[...]
    gate = _mm(h, wg_ref[...])
    up = _mm(h, wu_ref[...])
    part = _mm(_silu(gate) * up, wd_ref[...])

    @pl.when(j == 0)
    def _():
        o_ref[...] = part

    @pl.when(j > 0)
    def _():
        o_ref[...] = o_ref[...] + part

    @pl.when(j == pl.num_programs(1) - 1)
    def _():
        o_ref[...] = x_ref[...] + g_ref[...] * o_ref[...]


def _swiglu_call(h, wg, wu, wd, x, g, rows_per_batch, tm, tf=256):
    m, d = x.shape
    f = wg.shape[1]
    g_in, g_spec = _gate_spec(g, rows_per_batch, tm, d, False)
    return pl.pallas_call(
        _swiglu_kernel,
        grid=(m // tm, f // tf),
        in_specs=[
            pl.BlockSpec((tm, d), lambda i, j: (i, 0)),
            pl.BlockSpec((d, tf), lambda i, j: (0, j)),
            pl.BlockSpec((d, tf), lambda i, j: (0, j)),
            pl.BlockSpec((tf, d), lambda i, j: (j, 0)),
            pl.BlockSpec((tm, d), lambda i, j: (i, 0), pipeline_mode=pl.Buffered(1)),
            g_spec,
        ],
        out_specs=pl.BlockSpec((tm, d), lambda i, j: (i, 0)),
        out_shape=jax.ShapeDtypeStruct((m, d), F32),
        compiler_params=_cparams(("arbitrary", "arbitrary")),
        name="swiglu",
    )(h, wg, wu, wd, x, g_in)


def _router_kernel(h_ref, w_ref, b_ref, id_ref, wt_ref):
    logits = _mm(h_ref[...], w_ref[...]) + b_ref[...]
    lane = lax.broadcasted_iota(I32, logits.shape, 1)
    logits = jnp.where(lane < N_EXPERTS, logits, -jnp.inf)
    m1 = jnp.max(logits, axis=-1, keepdims=True)
    i1 = jnp.min(jnp.where(logits == m1, lane, LANE), axis=-1, keepdims=True)
    rest = jnp.where(lane == i1, -jnp.inf, logits)
    m2 = jnp.max(rest, axis=-1, keepdims=True)
    i2 = jnp.min(jnp.where(rest == m2, lane, LANE), axis=-1, keepdims=True)
    e = jnp.exp(m2 - m1)
    w1 = 1.0 / (1.0 + e)
    w2 = e / (1.0 + e)
    id_ref[...] = jnp.where(lane == 0, i1, jnp.where(lane == 1, i2, 0))
    wt_ref[...] = jnp.where(lane == 0, w1, jnp.where(lane == 1, w2, 0.0))


def _router_call(h, w_router, b_router, tm):
    m, d = h.shape
    ne = w_router.shape[1]
    w_pad = jnp.pad(w_router, ((0, 0), (0, LANE - ne)))
    b_pad = jnp.pad(b_router.reshape(1, ne), ((0, 0), (0, LANE - ne)))
    spec = pl.BlockSpec((tm, LANE), lambda i: (i, 0))
    ids, wts = pl.pallas_call(
        _router_kernel,
        grid=(m // tm,),
        in_specs=[
            pl.BlockSpec((tm, d), lambda i: (i, 0)),
            pl.BlockSpec((d, LANE), lambda i: (0, 0)),
            pl.BlockSpec((1, LANE), lambda i: (0, 0)),
        ],
        out_specs=[spec, spec],
        out_shape=[jax.ShapeDtypeStruct((m, LANE), I32), jax.ShapeDtypeStruct((m, LANE), F32)],
        compiler_params=_cparams(("arbitrary",)),
        name="router",
    )(h, w_pad, b_pad)
    return ids[:, :2], wts[:, :2]


def _rank_kernel(id_ref, rank_ref, cnt_ref, carry_ref):
    i = pl.program_id(0)
    tm = id_ref.shape[0]

    @pl.when(i == 0)
    def _():
        carry_ref[...] = jnp.zeros_like(carry_ref)

    ids = id_ref[...]
    lane = lax.broadcasted_iota(I32, ids.shape, 1)
    oh0 = jnp.where(lane == ids[:, 0:1], 1.0, 0.0)
    oh1 = jnp.where(lane == ids[:, 1:2], 1.0, 0.0)
    sel = oh0 + oh1
    r = lax.broadcasted_iota(I32, (tm, tm), 0)
    c = lax.broadcasted_iota(I32, (tm, tm), 1)
    strict_lower = jnp.where(c < r, 1.0, 0.0).astype(BF16)
    pre = _dot(strict_lower, sel.astype(BF16)) + carry_ref[...]
    rank0 = jnp.sum(oh0 * pre, axis=-1, keepdims=True)
    rank1 = jnp.sum(oh1 * pre, axis=-1, keepdims=True)
    rank_ref[...] = jnp.where(lane == 0, rank0, jnp.where(lane == 1, rank1, 0.0)).astype(I32)
    total = pre[tm - 1:tm, :] + sel[tm - 1:tm, :]
    carry_ref[...] = total
    cnt_ref[...] = total


def _rank_call(ids, tm=128):
    t = ids.shape[0]
    ids_pad = jnp.pad(ids, ((0, 0), (0, LANE - 2)), constant_values=-1)
    rank, cnt = pl.pallas_call(
        _rank_kernel,
        grid=(t // tm,),
        in_specs=[pl.BlockSpec((tm, LANE), lambda i: (i, 0))],
        out_specs=[pl.BlockSpec((tm, LANE), lambda i: (i, 0)),
                   pl.BlockSpec((1, LANE), lambda i: (0, 0))],
        out_shape=[jax.ShapeDtypeStruct((t, LANE), I32), jax.ShapeDtypeStruct((1, LANE), F32)],
        scratch_shapes=[pltpu.VMEM((1, LANE), F32)],
        compiler_params=_cparams(("arbitrary",)),
        name="rank",
    )(ids_pad)
    return rank[:, :2], cnt[0, :N_EXPERTS].astype(I32)


def _row_copy(src_ref, src_row, dst_ref, dst_row, sem):
    return pltpu.make_async_copy(src_ref.at[pl.ds(src_row, 1)], dst_ref.at[pl.ds(dst_row, 1)], sem)


def _dispatch_kernel(dest_ref, h_ref, hs_in_ref, hs_ref, sem):
    del hs_in_ref
    n = h_ref.shape[0]

    def start(r, carry):
        _row_copy(h_ref, r, hs_ref, dest_ref[0, 0, 2 * r], sem).start()
        _row_copy(h_ref, r, hs_ref, dest_ref[0, 0, 2 * r + 1], sem).start()
        return carry

    lax.fori_loop(0, n, start, 0)

    def wait(r, carry):
        _row_copy(h_ref, 0, hs_ref, 0, sem).wait()
        _row_copy(h_ref, 0, hs_ref, 0, sem).wait()
        return carry

    lax.fori_loop(0, n, wait, 0)


def _dispatch_call(h, dest, hs, tm):
    t, d = h.shape
    dest3 = dest.reshape(t // tm, 1, 2 * tm)
    return pl.pallas_call(
        _dispatch_kernel,
        grid=(t // tm,),
        in_specs=[
            pl.BlockSpec((1, 1, 2 * tm), lambda i: (i, 0, 0), memory_space=pltpu.SMEM),
            pl.BlockSpec((tm, d), lambda i: (i, 0)),
            pl.BlockSpec(memory_space=pl.ANY),
        ],
        out_specs=pl.BlockSpec(memory_space=pl.ANY),
        out_shape=jax.ShapeDtypeStruct(hs.shape, hs.dtype),
        scratch_shapes=[pltpu.SemaphoreType.DMA],
        input_output_aliases={2: 0},
        compiler_params=_cparams(("arbitrary",)),
        name="dispatch",
    )(dest3, h, hs)


def _grouped_kernel(te_tab, nr_tab, nv_tab, hs_ref, wg_ref, wu_ref, wd_ref, ys_ref,
                    x_scr, wg_scr, wu_scr, wd_scr):
    i, j = pl.program_id(0), pl.program_id(1)
    tm = hs_ref.shape[0]
    n_rows = nr_tab[i]

    @pl.when(n_rows > 0)
    def _():
        wg_scr[...] = wg_ref[...].astype(BF16)
        wu_scr[...] = wu_ref[...].astype(BF16)
        wd_scr[...] = wd_ref[...].astype(BF16)

    def ffn_rows(rows):
        @pl.when(j == 0)
        def _():
            x_scr[rows, :] = hs_ref[rows, :].astype(BF16)

        x = x_scr[rows, :]
        gate = _dot(x, wg_scr[...])
        up = _dot(x, wu_scr[...])
        part = _dot((_silu(gate) * up).astype(BF16), wd_scr[...])

        @pl.when(j == 0)
        def _():
            ys_ref[rows, :] = part

        @pl.when(j > 0)
        def _():
            ys_ref[rows, :] = ys_ref[rows, :] + part

    @pl.when(n_rows == tm)
    def _():
        ffn_rows(pl.ds(0, tm))

    for sub in range(tm // GROUP_SUB):
        rows = pl.ds(sub * GROUP_SUB, GROUP_SUB)

        @pl.when(jnp.logical_and(n_rows < tm, sub * GROUP_SUB < n_rows))
        def _():
            ffn_rows(rows)

        @pl.when(jnp.logical_and(sub * GROUP_SUB >= n_rows, j == 0))
        def _():
            ys_ref[rows, :] = jnp.zeros((GROUP_SUB, ys_ref.shape[1]), F32)


def _grouped_call(hs, tile_expert, tile_rows, n_valid, wg, wu, wd, tm, tf=256):
    r, d = hs.shape
    f = wg.shape[2]
    n_tiles, nf = r // tm, f // tf

    def row_map(i, j, te, nr, nv):
        return (jnp.minimum(i, nv[0] - 1), 0)

    def col_blk(i, j, nv):
        return jnp.where(i < nv[0], j, nf - 1)

    grid_spec = pltpu.PrefetchScalarGridSpec(
        num_scalar_prefetch=3,
        grid=(n_tiles, nf),
        in_specs=[
            pl.BlockSpec((tm, d), row_map, pipeline_mode=pl.Buffered(1)),
            pl.BlockSpec((None, d, tf), lambda i, j, te, nr, nv: (te[i], 0, col_blk(i, j, nv))),
            pl.BlockSpec((None, d, tf), lambda i, j, te, nr, nv: (te[i], 0, col_blk(i, j, nv))),
            pl.BlockSpec((None, tf, d), lambda i, j, te, nr, nv: (te[i], col_blk(i, j, nv), 0)),
        ],
        out_specs=pl.BlockSpec((tm, d), lambda i, j, te, nr, nv: (i, 0)),
        scratch_shapes=[pltpu.VMEM((tm, d), BF16), pltpu.VMEM((d, tf), BF16),
                        pltpu.VMEM((d, tf), BF16), pltpu.VMEM((tf, d), BF16)],
    )
    return pl.pallas_call(
        _grouped_kernel,
        grid_spec=grid_spec,
        out_shape=jax.ShapeDtypeStruct((r, d), F32),
        compiler_params=_cparams(("arbitrary", "arbitrary")),
        name="grouped_ffn",
    )(tile_expert, tile_rows, n_valid, hs, wg, wu, wd)


def _combine_kernel(dest_ref, wt_ref, x_ref, g_ref, ys_ref, o_ref, buf, sem):
    n = x_ref.shape[0]

    def start(r, carry):
        _row_copy(ys_ref, dest_ref[0, 0, 2 * r], buf.at[0], r, sem).start()
        _row_copy(ys_ref, dest_ref[0, 0, 2 * r + 1], buf.at[1], r, sem).start()
        return carry

    lax.fori_loop(0, n, start, 0)

    def wait(r, carry):
        _row_copy(ys_ref, 0, buf.at[0], 0, sem).wait()
        _row_copy(ys_ref, 0, buf.at[1], 0, sem).wait()
        return carry

    lax.fori_loop(0, n, wait, 0)
    wt = wt_ref[...]
    f = wt[:, 0:1] * buf[0] + wt[:, 1:2] * buf[1]
    o_ref[...] = x_ref[...] + g_ref[...] * f


def _combine_call(ys, dest, wts, x, g, rows_per_batch, tm):
    t, d = x.shape
    dest3 = dest.reshape(t // tm, 1, 2 * tm)
    wt_pad = jnp.pad(wts, ((0, 0), (0, LANE - 2)))
    if rows_per_batch % tm == 0:
        per = rows_per_batch // tm
        g_in, g_spec = g.reshape(-1, 1, d), pl.BlockSpec((None, 1, d), lambda i: (i // per, 0, 0))
    else:
        g_in, g_spec = jnp.repeat(g, rows_per_batch, axis=0), pl.BlockSpec((tm, d), lambda i: (i, 0))
    return pl.pallas_call(
        _combine_kernel,
        grid=(t // tm,),
        in_specs=[
            pl.BlockSpec((1, 1, 2 * tm), lambda i: (i, 0, 0), memory_space=pltpu.SMEM),
            pl.BlockSpec((tm, LANE), lambda i: (i, 0)),
            pl.BlockSpec((tm, d), lambda i: (i, 0)),
            g_spec,
            pl.BlockSpec(memory_space=pl.ANY),
        ],
        out_specs=pl.BlockSpec((tm, d), lambda i: (i, 0)),
        out_shape=jax.ShapeDtypeStruct((t, d), F32),
        scratch_shapes=[pltpu.VMEM((2, tm, d), F32), pltpu.SemaphoreType.DMA],
        compiler_params=_cparams(("arbitrary",)),
        name="combine",
    )(dest3, wt_pad, x, g_in, ys)


def _moe(h_groups, x_groups, g_groups, rows_per_batch, w_router, b_router, wg, wu, wd,
         tm_tok, tm_grp):
    d = h_groups[0].shape[1]
    routed = [_router_call(h, w_router, b_router, tm_tok) for h in h_groups]
    ids = jnp.concatenate([r[0] for r in routed], axis=0)
    n_tok = ids.shape[0]
    rank, counts = _rank_call(ids)
    padded = (counts + tm_grp - 1) // tm_grp * tm_grp
    ends = jnp.cumsum(padded)
    starts = ends - padded
    dest = starts[ids] + rank
    n_tiles = -(-2 * n_tok // tm_grp) + N_EXPERTS
    tile_row = jnp.arange(n_tiles, dtype=I32) * tm_grp
    tile_expert = jnp.minimum(jnp.sum(tile_row[:, None] >= ends[None, :], axis=1),
                              N_EXPERTS - 1).astype(I32)
    n_valid = (ends[-1] // tm_grp).astype(I32).reshape(1)
    tile_expert = jnp.where(tile_row < ends[-1], tile_expert,
                            tile_expert[jnp.maximum(n_valid[0] - 1, 0)])
    tile_rows = jnp.clip((starts + counts)[tile_expert] - tile_row, 0, tm_grp)
    tile_rows = jnp.where(tile_row < ends[-1], tile_rows, 0).astype(I32)
    hs = jnp.zeros((n_tiles * tm_grp, d), F32)
    off = 0
    for h in h_groups:
        n = h.shape[0]
        hs = _dispatch_call(h, dest[off:off + n], hs, tm_tok)
        off += n
    ys = _grouped_call(hs, tile_expert, tile_rows, n_valid, wg, wu, wd, tm_grp)
    outs, off = [], 0
    for (ids_g, wts_g), x, g, rpb in zip(routed, x_groups, g_groups, rows_per_batch):
        n = x.shape[0]
        outs.append(_combine_call(ys, dest[off:off + n], wts_g, x, g, rpb, tm_tok))
        off += n
    return outs


PAGES_PER_STEP = 8


def _rows2d(ref3):
    k, h, d = ref3.shape
    return ref3[...].reshape(k * h, d)


def _new_rows(nkv_ref, which, n_rows):
    x = nkv_ref[:, which].reshape(n_rows, HEAD_DIM)
    return jnp.concatenate([x, jnp.zeros((LANE - n_rows, HEAD_DIM), F32)], axis=0)


def _head_match(n_rows, n_cols, n_heads, n_tok):
    r = lax.broadcasted_iota(I32, (n_rows, n_cols), 0)
    c = lax.broadcasted_iota(I32, (n_rows, n_cols), 1)
    hshift, tshift = int(math.log2(n_heads)), int(math.log2(n_tok))
    same = (lax.shift_right_logical(r, tshift) & (n_heads - 1)) == (c & (n_heads - 1))
    return same, lax.shift_right_logical(c, hshift), r & (n_tok - 1)


def _page_specs(n_pages, page, n_heads, which, n_steps, active_phase):
    def make(slot):
        def index(b, t, pt):
            a = t % n_steps
            if active_phase is not None:
                mine = (t // n_steps) == active_phase
                a = jnp.where(mine, a, 1 if active_phase == 1 else n_steps - 1)
            pos = n_pages - 1 - ((jnp.maximum(a, 1) - 1) * PAGES_PER_STEP + slot)
            return (0, pt[b, pos], 0, which, 0, 0)
        return pl.BlockSpec((None, None, page, None, n_heads, HEAD_DIM), index)
    return [make(s) for s in range(PAGES_PER_STEP)]


def _softmax_stats(s, m_prev, l_prev):
    m_new = jnp.maximum(m_prev, jnp.max(s, axis=-1, keepdims=True))
    l_new = jnp.exp(m_prev - m_new) * l_prev + jnp.sum(jnp.exp(s - m_new), axis=-1, keepdims=True)
    return m_new, l_new


def _fox_sample_kernel(pt_ref, q_ref, nkv_ref, nlf_ref, rowsuf_ref, *refs,
                       n_heads, n_tok, scale, n_steps):
    del pt_ref
    pps = PAGES_PER_STEP
    k_refs, v_refs, lf_refs = refs[:pps], refs[pps:2 * pps], refs[2 * pps:3 * pps]
    o_ref, s_scr, snew_scr, m_scr, l_scr, acc_scr, carry_scr = refs[3 * pps:]
    t = pl.program_id(1)
    n_rows = n_heads * n_tok
    hshift = int(math.log2(n_heads))
    q = q_ref[...].astype(BF16)

    def score(k2d, lfv, is_new, carry):
        n_cols = k2d.shape[0]
        n_chunks = n_cols // LANE
        l1 = lax.broadcasted_iota(I32, (LANE, LANE), 0)
        l2 = lax.broadcasted_iota(I32, (LANE, LANE), 1)
        same_head = (l1 & (n_heads - 1)) == (l2 & (n_heads - 1))
        later_key = lax.shift_right_logical(l1, hshift) > lax.shift_right_logical(l2, hshift)
        within_m = jnp.where(same_head & later_key, 1.0, 0.0).astype(BF16)
        total_m = jnp.where(same_head, 1.0, 0.0).astype(BF16)
        within = _dot_sel(within_m, lfv, sel_first=False)
        rowtot = _dot_sel(total_m, lfv, sel_first=False)
        ridx = lax.broadcasted_iota(I32, lfv.shape, 0)
        rowsuf = jnp.zeros(lfv.shape, F32)
        for a in range(1, lfv.shape[0]):
            rowsuf = rowsuf + jnp.where(ridx < a, rowtot[a:a + 1, :], 0.0)
        suf = within + rowsuf + carry
        carry = carry + jnp.sum(rowtot, axis=0, keepdims=True)
        bias = (jnp.concatenate([suf[a:a + 1, :] for a in range(n_chunks)], axis=1)
                if n_chunks > 1 else suf[0:1, :])
        s = _dot(q, k2d.astype(BF16), NT) * scale + (bias - rowsuf_ref[...])
        same, key, tok = _head_match(n_rows, n_cols, n_heads, n_tok)
        valid = (same & (key <= tok)) if is_new else same
        return jnp.where(valid, s, NEG), carry

    def weights(s, m, inv_l):
        return (jnp.exp(s - m) * inv_l).astype(BF16)

    @pl.when(t == 0)
    def _():
        s, carry = score(_new_rows(nkv_ref, 0, n_rows), nlf_ref[...], True,
                         jnp.zeros(carry_scr.shape, F32))
        snew_scr[...] = s
        m, l = _softmax_stats(s, jnp.full(m_scr.shape, NEG, F32), jnp.zeros(l_scr.shape, F32))
        m_scr[...], l_scr[...], carry_scr[...] = m, l, carry

    @pl.when(jnp.logical_and(t > 0, t < n_steps))
    def _():
        m, l, carry = m_scr[...], l_scr[...], carry_scr[...]
        for i in range(pps):
            s, carry = score(_rows2d(k_refs[i]), lf_refs[i][...], False, carry)
            s_scr[(t - 1) * pps + i] = s
            m, l = _softmax_stats(s, m, l)
        m_scr[...], l_scr[...], carry_scr[...] = m, l, carry

    @pl.when(t == n_steps)
    def _():
        pw = weights(snew_scr[...], m_scr[...], 1.0 / l_scr[...])
        acc_scr[...] = _dot(pw, _new_rows(nkv_ref, 1, n_rows).astype(BF16))

    @pl.when(t > n_steps)
    def _():
        m, inv_l, acc = m_scr[...], 1.0 / l_scr[...], acc_scr[...]
        for i in range(pps):
            pw = weights(s_scr[(t - n_steps - 1) * pps + i], m, inv_l)
            acc = acc + _dot(pw, _rows2d(v_refs[i]).astype(BF16))
        acc_scr[...] = acc

    @pl.when(t == 2 * n_steps - 1)
    def _():
        o_ref[...] = acc_scr[...]


def _sample_rows(x, batch, n_tok, n_heads):
    return (x.reshape(batch, n_tok, n_heads, HEAD_DIM).transpose(0, 2, 1, 3)
            .reshape(batch, n_heads * n_tok, HEAD_DIM))


def _sample_unrows(o, batch, n_tok, n_heads):
    return (o.reshape(batch, n_heads, n_tok, HEAD_DIM).transpose(0, 2, 1, 3)
            .reshape(batch * n_tok, n_heads * HEAD_DIM))


def _fox_sample_call(q, new_kv, new_lf, new_suf, cache_kv, cache_lf, page_table, n_tok):
    batch, n_pages = page_table.shape
    n_layers, n_phys, page, _, n_heads, _ = cache_kv.shape
    pps = PAGES_PER_STEP
    assert n_pages % pps == 0
    n_steps = 1 + n_pages // pps
    n_rows = n_heads * n_tok
    n_cols = page * n_heads
    lf_rows = n_cols // LANE
    q_rows = _sample_rows(q, batch, n_tok, n_heads)
    nkv = new_kv.reshape(batch, n_tok, 2, n_heads, HEAD_DIM)
    nlf = jnp.pad(new_lf.reshape(batch, 1, n_rows), ((0, 0), (0, 7), (0, LANE - n_rows)))
    rowsuf = new_suf.reshape(batch, n_tok, n_heads).transpose(0, 2, 1).reshape(batch, n_rows, 1)
    lf_pages = cache_lf.reshape(n_layers, n_phys, lf_rows, LANE)

    def lf_spec(slot):
        def index(b, t, pt):
            a = jnp.where(t < n_steps, t, n_steps - 1)
            pos = n_pages - 1 - ((jnp.maximum(a, 1) - 1) * pps + slot)
            return (0, pt[b, pos], 0, 0)
        return pl.BlockSpec((None, None, lf_rows, LANE), index)

    grid_spec = pltpu.PrefetchScalarGridSpec(
        num_scalar_prefetch=1,
        grid=(batch, 2 * n_steps),
        in_specs=[
            pl.BlockSpec((None, n_rows, HEAD_DIM), lambda b, t, pt: (b, 0, 0)),
            pl.BlockSpec((None, n_tok, 2, n_heads, HEAD_DIM), lambda b, t, pt: (b, 0, 0, 0, 0)),
            pl.BlockSpec((None, 8, LANE), lambda b, t, pt: (b, 0, 0)),
            pl.BlockSpec((None, n_rows, 1), lambda b, t, pt: (b, 0, 0)),
            *_page_specs(n_pages, page, n_heads, 0, n_steps, 0),
            *_page_specs(n_pages, page, n_heads, 1, n_steps, 1),
            *[lf_spec(s) for s in range(pps)],
        ],
        out_specs=pl.BlockSpec((None, n_rows, HEAD_DIM), lambda b, t, pt: (b, 0, 0)),
        scratch_shapes=[pltpu.VMEM((n_pages, n_rows, n_cols), F32),
                        pltpu.VMEM((n_rows, LANE), F32),
                        pltpu.VMEM((n_rows, 1), F32), pltpu.VMEM((n_rows, 1), F32),
                        pltpu.VMEM((n_rows, HEAD_DIM), F32), pltpu.VMEM((1, LANE), F32)],
    )
    o = pl.pallas_call(
        functools.partial(_fox_sample_kernel, n_heads=n_heads, n_tok=n_tok,
                          scale=HEAD_DIM ** -0.5, n_steps=n_steps),
        grid_spec=grid_spec,
        out_shape=jax.ShapeDtypeStruct((batch, n_rows, HEAD_DIM), F32),
        compiler_params=_cparams(("arbitrary", "arbitrary")),
        name="fox_sample",
    )(page_table, q_rows, nkv, nlf, rowsuf, *([cache_kv] * (2 * pps)), *([lf_pages] * pps))
    return _sample_unrows(o, batch, n_tok, n_heads)


def _diff_sample_kernel(pt_ref, rb_ref, q_ref, nkv_ref, lam_ref, *refs,
                        n_heads, n_tok, page, scale, lam_init, n_steps):
    del pt_ref
    pps = PAGES_PER_STEP
    k_refs, v_refs = refs[:pps], refs[pps:2 * pps]
    o_ref, s_scr, snew_scr, m_scr, l_scr, acc_scr = refs[2 * pps:]
    t = pl.program_id(1)
    n_rows = n_heads * n_tok
    n_cols = page * n_heads
    q2 = q_ref[...].astype(BF16)

    def rel_bias_cols():
        hh = lax.broadcasted_iota(I32, (LANE, n_cols), 0)
        cc = lax.broadcasted_iota(I32, (LANE, n_cols), 1)
        head_sel = jnp.where(hh == (cc & (n_heads - 1)), 1.0, 0.0).astype(BF16)
        return _dot_sel(head_sel, rb_ref[...], sel_first=False)

    def score(k2d, rb_cols, dist0, is_new):
        nc = k2d.shape[0]
        s = _dot(q2, k2d.astype(BF16), NT) * scale
        same, key, tok = _head_match(2 * n_rows, nc, n_heads, n_tok)
        if dist0 is None:
            bias = rb_cols[N_BUCKETS - 1:N_BUCKETS, :nc]
        else:
            bucket = _t5_bucket(dist0 + tok - key)
            bias = jnp.zeros(s.shape, F32)
            for b in range(N_BUCKETS):
                bias = jnp.where(bucket == b, rb_cols[b:b + 1, :nc], bias)
        valid = (same & (key <= tok)) if is_new else same
        return jnp.where(valid, s + bias, NEG)

    def weights(s, m, inv_l, lam):
        pn = jnp.exp(s - m) * inv_l
        return (pn[:n_rows] - lam * pn[n_rows:]).astype(BF16)

    @pl.when(t == 0)
    def _():
        s = score(_new_rows(nkv_ref, 0, n_rows), rel_bias_cols(), 0, True)
        snew_scr[...] = s
        m, l = _softmax_stats(s, jnp.full(m_scr.shape, NEG, F32), jnp.zeros(l_scr.shape, F32))
        m_scr[...], l_scr[...] = m, l

    @pl.when(t == 1)
    def _():
        rb_cols = rel_bias_cols()
        m, l = m_scr[...], l_scr[...]
        for i in range(pps):
            s = score(_rows2d(k_refs[i]), rb_cols, page if i == 0 else None, False)
            s_scr[i] = s
            m, l = _softmax_stats(s, m, l)
        m_scr[...], l_scr[...] = m, l

    @pl.when(jnp.logical_and(t > 1, t < n_steps))
    def _():
        rb_cols = rel_bias_cols()
        m, l = m_scr[...], l_scr[...]
        for i in range(pps):
            s = score(_rows2d(k_refs[i]), rb_cols, None, False)
            s_scr[(t - 1) * pps + i] = s
            m, l = _softmax_stats(s, m, l)
        m_scr[...], l_scr[...] = m, l

    @pl.when(t == n_steps)
    def _():
        pw = weights(snew_scr[...], m_scr[...], 1.0 / l_scr[...], _diff_lambda(lam_ref, lam_init))
        acc_scr[...] = _dot(pw, _new_rows(nkv_ref, 1, n_rows).astype(BF16))

    @pl.when(t > n_steps)
    def _():
        m, inv_l, acc = m_scr[...], 1.0 / l_scr[...], acc_scr[...]
        lam = _diff_lambda(lam_ref, lam_init)
        for i in range(pps):
            pw = weights(s_scr[(t - n_steps - 1) * pps + i], m, inv_l, lam)
            acc = acc + _dot(pw, _rows2d(v_refs[i]).astype(BF16))
        acc_scr[...] = acc

    @pl.when(t == 2 * n_steps - 1)
    def _():
        o = acc_scr[...]
        o_ref[...] = o * lax.rsqrt(jnp.mean(o * o, axis=-1, keepdims=True) + EPS) * (1.0 - lam_init)


def _diff_sample_call(q, new_kv, cache_kv, page_table, rel_bias, lam_diff, n_tok, lam_init):
    batch, n_pages = page_table.shape
    _, _, page, _, n_heads, _ = cache_kv.shape
    pps = PAGES_PER_STEP
    assert n_pages % pps == 0
    assert page >= MAX_DISTANCE
    n_steps = 1 + n_pages // pps
    n_rows = n_heads * n_tok
    n_cols = page * n_heads
    q_rows = _sample_rows(q, batch, n_tok, n_heads)
    lane = jnp.arange(HEAD_DIM) < DIFF_DK
    q2 = jnp.concatenate([jnp.where(lane, q_rows, 0.0), jnp.where(lane, 0.0, q_rows)], axis=1)
    nkv = new_kv.reshape(batch, n_tok, 2, n_heads, HEAD_DIM)
    rb_pad = jnp.pad(rel_bias, ((0, 0), (0, LANE - n_heads)))
    grid_spec = pltpu.PrefetchScalarGridSpec(
        num_scalar_prefetch=1,
        grid=(batch, 2 * n_steps),
        in_specs=[
            pl.BlockSpec(rb_pad.shape, lambda b, t, pt: (0, 0)),
            pl.BlockSpec((None, 2 * n_rows, HEAD_DIM), lambda b, t, pt: (b, 0, 0)),
            pl.BlockSpec((None, n_tok, 2, n_heads, HEAD_DIM), lambda b, t, pt: (b, 0, 0, 0, 0)),
            pl.BlockSpec(lam_diff.shape, lambda b, t, pt: (0, 0)),
            *_page_specs(n_pages, page, n_heads, 0, n_steps, 0),
            *_page_specs(n_pages, page, n_heads, 1, n_steps, 1),
        ],
        out_specs=pl.BlockSpec((None, n_rows, HEAD_DIM), lambda b, t, pt: (b, 0, 0)),
        scratch_shapes=[pltpu.VMEM((n_pages, 2 * n_rows, n_cols), F32),
                        pltpu.VMEM((2 * n_rows, LANE), F32),
                        pltpu.VMEM((2 * n_rows, 1), F32), pltpu.VMEM((2 * n_rows, 1), F32),
                        pltpu.VMEM((n_rows, HEAD_DIM), F32)],
    )
    o = pl.pallas_call(
        functools.partial(_diff_sample_kernel, n_heads=n_heads, n_tok=n_tok, page=page,
                          scale=DIFF_DK ** -0.5, lam_init=lam_init, n_steps=n_steps),
        grid_spec=grid_spec,
        out_shape=jax.ShapeDtypeStruct((batch, n_rows, HEAD_DIM), F32),
        compiler_params=_cparams(("arbitrary", "arbitrary")),
        name="diff_sample",
    )(page_table, rb_pad, q2, nkv, lam_diff, *([cache_kv] * (2 * pps)))
    return _sample_unrows(o, batch, n_tok, n_heads)


def _sb_sample_kernel(pt_ref, q_ref, nkv_ref, gat_ref, sct_ref, *refs, n_heads, n_tok, scale):
    del pt_ref
    pps = PAGES_PER_STEP
    k_refs, v_refs = refs[:pps], refs[pps:2 * pps]
    o_ref, acc_scr, r_scr = refs[2 * pps:]
    t = pl.program_id(1)
    n_rows = n_heads * n_tok
    q = q_ref[...].astype(BF16)

    def later_keys():
        a_ = lax.broadcasted_iota(I32, (LANE, LANE), 0)
        b_ = lax.broadcasted_iota(I32, (LANE, LANE), 1)
        return jnp.where(a_ > b_, 1.0, 0.0).astype(BF16)

    def attend(k2d, v2d, valid, later, acc, r):
        nc = k2d.shape[0]
        z = _dot(q, k2d.astype(BF16), NT) * scale
        ls = _log_sigmoid(z)
        lr = jnp.where(valid, ls - z, 0.0)
        compact = _dot_sel2(lr, gat_ref[:nc, :])
        suf = _dot_sel(later, compact, sel_first=False)
        after = _dot_sel2(suf, sct_ref[:, :nc]) + r
        a = jnp.where(valid, jnp.exp(ls + after), 0.0)
        return (acc + _dot(a.astype(BF16), v2d.astype(BF16)),
                r + jnp.sum(compact, axis=-1, keepdims=True))

    @pl.when(t == 0)
    def _():
        same, key, tok = _head_match(n_rows, LANE, n_heads, n_tok)
        acc, r = attend(_new_rows(nkv_ref, 0, n_rows), _new_rows(nkv_ref, 1, n_rows),
                        same & (key < tok), later_keys(),
                        jnp.zeros(acc_scr.shape, F32), jnp.zeros(r_scr.shape, F32))
        acc_scr[...] = acc
        r_scr[...] = r

    @pl.when(t > 0)
    def _():
        same, _, _ = _head_match(n_rows, k_refs[0].shape[0] * n_heads, n_heads, n_tok)
        later = later_keys()
        acc, r = acc_scr[...], r_scr[...]
        for i in range(pps):
            acc, r = attend(_rows2d(k_refs[i]), _rows2d(v_refs[i]), same, later, acc, r)
        acc_scr[...] = acc
        r_scr[...] = r

    @pl.when(t == pl.num_programs(1) - 1)
    def _():
        o_ref[...] = acc_scr[...]


def _sb_sample_call(q, new_kv, cache_kv, page_table, n_tok):
    batch, n_pages = page_table.shape
    _, _, page, _, n_heads, _ = cache_kv.shape
    pps = PAGES_PER_STEP
    assert n_pages % pps == 0 and page <= LANE
    n_steps = 1 + n_pages // pps
    n_rows = n_heads * n_tok
    n_cols = page * n_heads
    q_rows = _sample_rows(q, batch, n_tok, n_heads)
    nkv = new_kv.reshape(batch, n_tok, 2, n_heads, HEAD_DIM)
    key_of_col = jnp.arange(n_cols, dtype=I32) // n_heads
    gat = (key_of_col[:, None] == jnp.arange(LANE, dtype=I32)[None, :]).astype(BF16)
    grid_spec = pltpu.PrefetchScalarGridSpec(
        num_scalar_prefetch=1,
        grid=(batch, n_steps),
        in_specs=[
            pl.BlockSpec((None, n_rows, HEAD_DIM), lambda b, t, pt: (b, 0, 0)),
            pl.BlockSpec((None, n_tok, 2, n_heads, HEAD_DIM), lambda b, t, pt: (b, 0, 0, 0, 0)),
            pl.BlockSpec((n_cols, LANE), lambda b, t, pt: (0, 0)),
            pl.BlockSpec((LANE, n_cols), lambda b, t, pt: (0, 0)),
            *_page_specs(n_pages, page, n_heads, 0, n_steps, None),
            *_page_specs(n_pages, page, n_heads, 1, n_steps, None),
        ],
        out_specs=pl.BlockSpec((None, n_rows, HEAD_DIM), lambda b, t, pt: (b, 0, 0)),
        scratch_shapes=[pltpu.VMEM((n_rows, HEAD_DIM), F32), pltpu.VMEM((n_rows, 1), F32)],
    )
    o = pl.pallas_call(
        functools.partial(_sb_sample_kernel, n_heads=n_heads, n_tok=n_tok, scale=HEAD_DIM ** -0.5),
        grid_spec=grid_spec,
        out_shape=jax.ShapeDtypeStruct((batch, n_rows, HEAD_DIM), F32),
        compiler_params=_cparams(("arbitrary", "arbitrary")),
        name="sb_sample",
    )(page_table, q_rows, nkv, gat, gat.T, *([cache_kv] * (2 * pps)))
    return _sample_unrows(o, batch, n_tok, n_heads)


TM_PROMPT = 1024
TQ_PROMPT = 512
TM_TOKEN = 128
TM_GROUP = 1024
GROUP_SUB = 256


def kernel(x_prompt, x_sample, cache_fox_kv, cache_fox_logf, cache_diff_kv, cache_sb_kv, page_table,
           c_prompt, c_sample, w_ada, b_ada, g_norm, w_in_even, b_forget, g_q_fox, g_k_fox,
           g_q_diff, g_k_diff, lam_diff, rel_bias, w_out_even, w_gate, w_up, w_down,
           w_in_odd, w_out_odd, w_router, b_router, w_gate_exp, w_up_exp, w_down_exp):
    bp, seq, d = x_prompt.shape
    bs, n_tok, _ = x_sample.shape
    h_fox = cache_fox_kv.shape[4]
    h_diff = cache_diff_kv.shape[4]
    h_sb = cache_sb_kv.shape[4]
    fox_w, diff_w, sb_w = h_fox * HEAD_DIM, h_diff * HEAD_DIM, h_sb * HEAD_DIM
    mp, ms = bp * seq, bs * n_tok

    n_c = bp + bs
    c_all = jnp.pad(jnp.concatenate([c_prompt, c_sample], axis=0), ((0, -n_c % 8), (0, 0)))
    mods = _ada_call(c_all, w_ada, b_ada)

    def mod6(layer, lo, hi):
        return [mods[layer, lo:hi, k * d:(k + 1) * d] for k in range(6)]

    xp = x_prompt.reshape(mp, d)
    xs = x_sample.reshape(ms, d)

    lam_init0 = 0.8 - 0.6 * math.exp(-0.3 * 0)
    w_in0 = w_in_even[0]
    w_fg = w_in0[:, 3 * fox_w:3 * fox_w + h_fox]
    w_diff_in = w_in0[:, 3 * fox_w + h_fox:]
    gain_fox = jnp.concatenate([jnp.tile(g_q_fox[0], h_fox), jnp.tile(g_k_fox[0], h_fox),
                                jnp.ones((fox_w,), F32)]).reshape(1, 3 * fox_w)
    gain_diff = jnp.concatenate([jnp.tile(g_q_diff[0], 2 * h_diff), jnp.tile(g_k_diff[0], 2 * h_diff),
                                 jnp.ones((diff_w,), F32)]).reshape(1, 3 * diff_w)
    tt = _bias_tile_call(rel_bias, TQ_PROMPT)

    sh1, sc1, g1, sh2, sc2, g2 = mod6(0, 0, bp)
    hp = _normmod_call(xp, g_norm[0, 0], sc1, sh1, BF16, seq, TM_PROMPT)
    fox_qkv, fox_kv_p = _inproj_call(hp, w_in0, 0, 3 * fox_w, gain_fox, HEAD_DIM, True, TM_PROMPT)
    diff_qkv, diff_kv_p = _inproj_call(hp, w_diff_in, 0, 3 * diff_w, gain_diff, DIFF_DK, True,
                                       TM_PROMPT)
    logf_p, ck_p, _ = _logf_call(hp, w_fg, b_forget[0], seq, TQ_PROMPT)
    ck_rows = ck_p.reshape(bp, seq, h_fox).transpose(0, 2, 1)
    o_fox = _fox_prompt_call(fox_qkv, jnp.repeat(ck_p, HEAD_DIM, axis=1), ck_rows, bp, seq, h_fox,
                             TQ_PROMPT)
    o_diff = _diff_prompt_call(diff_qkv, tt, lam_diff[0], bp, seq, h_diff, TQ_PROMPT, lam_init0)
    xp = _outproj_call([o_fox, o_diff], w_out_even[0], xp, g1, seq, TM_PROMPT)
    hp = _normmod_call(xp, g_norm[0, 1], sc2, sh2, BF16, seq, TM_PROMPT)
    xp = _swiglu_call(hp, w_gate[0], w_up[0], w_down[0], xp, g2, seq, TM_PROMPT)

    sh1, sc1, g1, sh2, sc2, g2 = mod6(0, bp, bp + bs)
    hs_ = _normmod_call(xs, g_norm[0, 0], sc1, sh1, F32, n_tok, ms)
    fox_s = _inproj_call(hs_, w_in0, 0, 3 * fox_w, gain_fox, HEAD_DIM, False, ms)
    diff_s = _inproj_call(hs_, w_diff_in, 0, 3 * diff_w, gain_diff, DIFF_DK, False, ms)
    logf_s, _, suf_s = _logf_call(hs_, w_fg, b_forget[0], n_tok, ms)
    fox_kv_s, diff_kv_s = fox_s[:, fox_w:], diff_s[:, diff_w:]
    o_fox = _fox_sample_call(fox_s[:, :fox_w], fox_kv_s, logf_s, suf_s, cache_fox_kv, cache_fox_logf,
                             page_table, n_tok)
    o_diff = _diff_sample_call(diff_s[:, :diff_w], diff_kv_s, cache_diff_kv, page_table, rel_bias,
                               lam_diff[0], n_tok, lam_init0)
    xs = _outproj_call([o_fox, o_diff], w_out_even[0], xs, g1, n_tok, ms)
    hs_ = _normmod_call(xs, g_norm[0, 1], sc2, sh2, F32, n_tok, ms)
    xs = _swiglu_call(hs_, w_gate[0], w_up[0], w_down[0], xs, g2, n_tok, ms)

    w_in1 = w_in_odd[0]
    sh1, sc1, g1, sh2p, sc2p, g2p = mod6(1, 0, bp)
    hp = _normmod_call(xp, g_norm[1, 0], sc1, sh1, BF16, seq, TM_PROMPT)
    sb_qkv, sb_kv_p = _inproj_call(hp, w_in1, 0, 3 * sb_w, None, HEAD_DIM, True, TM_PROMPT)
    o_sb = _sb_prompt_call(sb_qkv, bp, seq, h_sb, TQ_PROMPT, 4)
    xp = _outproj_call([o_sb], w_out_odd[0], xp, g1, seq, TM_PROMPT)
    hp2 = _normmod_call(xp, g_norm[1, 1], sc2p, sh2p, F32, seq, TM_PROMPT)

    sh1, sc1, g1, sh2s, sc2s, g2s = mod6(1, bp, bp + bs)
    hs_ = _normmod_call(xs, g_norm[1, 0], sc1, sh1, F32, n_tok, ms)
    sb_s = _inproj_call(hs_, w_in1, 0, 3 * sb_w, None, HEAD_DIM, False, ms)
    sb_kv_s = sb_s[:, sb_w:]
    o_sb = _sb_sample_call(sb_s[:, :sb_w], sb_kv_s, cache_sb_kv, page_table, n_tok)
    xs = _outproj_call([o_sb], w_out_odd[0], xs, g1, n_tok, ms)
    hs2 = _normmod_call(xs, g_norm[1, 1], sc2s, sh2s, F32, n_tok, ms)

    yp, ys = _moe([hp2, hs2], [xp, xs], [g2p, g2s], [seq, n_tok], w_router[0], b_router[0],
                  w_gate_exp[0], w_up_exp[0], w_down_exp[0], TM_TOKEN, TM_GROUP)

    return (
        yp.reshape(bp, seq, d),
        ys.reshape(bs, n_tok, d),
        fox_kv_p.reshape(1, bp, seq, 2, h_fox, HEAD_DIM),
        logf_p.reshape(1, bp, seq, h_fox),
        diff_kv_p.reshape(1, bp, seq, 2, h_diff, HEAD_DIM),
        sb_kv_p.reshape(1, bp, seq, 2, h_sb, HEAD_DIM),
        fox_kv_s.reshape(1, bs, n_tok, 2, h_fox, HEAD_DIM),
        logf_s.reshape(1, bs, n_tok, h_fox),
        diff_kv_s.reshape(1, bs, n_tok, 2, h_diff, HEAD_DIM),
        sb_kv_s.reshape(1, bs, n_tok, 2, h_sb, HEAD_DIM),
    )
```
